```python
import math
import jax, jax.numpy as jnp
from jax import lax
import numpy as np

D_MODEL = 1024
BATCH = 16
SEQ = 2048
DEPTH = 2
DEC_BATCH = 32
DEC_SEQ = 2048
PAST_LEN = 128

BRANCH_WIDTH = 512
N_BRANCHES = 5
DIFF_HEADS = 4
DIFF_HEAD_DIM = 64
DIFF_V_DIM = 128
PARTIAL_ROPE_DIM = DIFF_HEAD_DIM // 4
ROPE_THETA = 500000.0
SC_KERNEL = 3
CONF_KERNEL = 31
MLA_HEADS = 4
MLA_NOPE = 64
MLA_ROPE = 32
MLA_V = 128
MLA_Q_RANK = 256
MLA_KV_RANK = 128
MLA_ROPE_THETA = 10000.0
MEM_HEADS = 4
MEM_HEAD_DIM = 128
N_MEM = 256
Q_BLOCK = 128
EPS = 1e-6

IN_SPLITS = (
    DIFF_HEADS * 2 * DIFF_HEAD_DIM, DIFF_HEADS * 2 * DIFF_HEAD_DIM, DIFF_HEADS * DIFF_V_DIM, BRANCH_WIDTH,
    BRANCH_WIDTH, BRANCH_WIDTH, BRANCH_WIDTH, BRANCH_WIDTH,
    2 * BRANCH_WIDTH, BRANCH_WIDTH,
    MLA_Q_RANK, MLA_KV_RANK, MLA_ROPE, BRANCH_WIDTH,
    MEM_HEADS * MEM_HEAD_DIM, BRANCH_WIDTH,
    N_BRANCHES * D_MODEL,
)
IN_COLS = sum(IN_SPLITS)

kernel_name = "hybrid_parallel_gated_encoder"


def _rms_norm(x, g):
    xf = x.astype(jnp.float32)
    y = xf * lax.rsqrt(jnp.mean(xf * xf, axis=-1, keepdims=True) + EPS)
    return (y * g.astype(jnp.float32)).astype(x.dtype)


def _layer_norm(x, g, b):
    xf = x.astype(jnp.float32)
    mu = jnp.mean(xf, axis=-1, keepdims=True)
    var = jnp.mean(jnp.square(xf - mu), axis=-1, keepdims=True)
    y = (xf - mu) * lax.rsqrt(var + EPS)
    return (y * g.astype(jnp.float32) + b.astype(jnp.float32)).astype(x.dtype)


def _rope(x, rd, theta):
    S = x.shape[1]
    inv = jnp.float32(theta) ** (-(jnp.arange(0, rd, 2, dtype=jnp.float32) / rd))
    ang = jnp.arange(S, dtype=jnp.float32)[:, None] * inv[None, :]
    cos = jnp.cos(ang)[None, :, None, :].astype(x.dtype)
    sin = jnp.sin(ang)[None, :, None, :].astype(x.dtype)
    x1 = x[..., : rd // 2]
    x2 = x[..., rd // 2: rd]
    return jnp.concatenate([x1 * cos - x2 * sin, x1 * sin + x2 * cos, x[..., rd:]], axis=-1)


def _dwconv(x, w):
    K, C = w.shape
    return lax.conv_general_dilated(
        x, w[:, None, :].astype(x.dtype), window_strides=(1,),
        padding=[(K // 2, K // 2)], dimension_numbers=("NWC", "WIO", "NWC"),
        feature_group_count=C)


def _to_blocks(t):
    B, S = t.shape[:2]
    nb = S // Q_BLOCK
    return jnp.moveaxis(t.reshape((B, nb, Q_BLOCK) + t.shape[2:]), 1, 0)


def _from_blocks(t):
    nb, B, qb = t.shape[:3]
    return jnp.moveaxis(t, 0, 1).reshape((B, nb * qb) + t.shape[3:])


def _diff_attention(q, k, v, lam):
    scale = q.shape[-1] ** -0.5

    def one(qb):
        s = jnp.einsum("bqhcd,bkhcd->bhcqk", qb, k).astype(jnp.float32) * scale
        p = jax.nn.softmax(s, axis=-1)
        a = p[:, :, 0] - lam * p[:, :, 1]
        return jnp.einsum("bhqk,bkhd->bqhd", a.astype(v.dtype), v)

    return _from_blocks(lax.map(one, _to_blocks(q)))


def _softmax_attention(q, k, v):
    scale = q.shape[-1] ** -0.5

    def one(qb):
        s = jnp.einsum("bqhd,bkhd->bhqk", qb, k).astype(jnp.float32) * scale
        p = jax.nn.softmax(s, axis=-1)
        return jnp.einsum("bhqk,bkhd->bqhd", p.astype(v.dtype), v)

    return _from_blocks(lax.map(one, _to_blocks(q)))


def _layer(x, mem, li, norm_pre, w_in, diff_lambda, diff_subln, sconv_w,
           conf_dw_w, conf_dw_b, conf_ln_g, conf_ln_b, mla_q_norm, mla_w_uq,
           mla_kv_norm, mla_w_ukv, mem_norm, mem_w_kv, w_branch, w_out, norm_post):
    B, S, _ = x.shape
    h = _rms_norm(x, norm_pre)
    proj = h @ w_in
    split_idx = [int(c) for c in np.cumsum(IN_SPLITS)[:-1]]
    (a_q, a_k, a_v, a_z, b_b, b_c, b_x, b_z, c_glu, c_z,
     d_cq, d_ckv, d_kr, d_z, e_q, e_z, gates) = jnp.split(proj, split_idx, axis=-1)

    q = _rope(a_q.reshape(B, S, DIFF_HEADS * 2, DIFF_HEAD_DIM), PARTIAL_ROPE_DIM, ROPE_THETA)
    k = _rope(a_k.reshape(B, S, DIFF_HEADS * 2, DIFF_HEAD_DIM), PARTIAL_ROPE_DIM, ROPE_THETA)
    q = q.reshape(B, S, DIFF_HEADS, 2, DIFF_HEAD_DIM)
    k = k.reshape(B, S, DIFF_HEADS, 2, DIFF_HEAD_DIM)
    v = a_v.reshape(B, S, DIFF_HEADS, DIFF_V_DIM)
    lam_init = 0.8 - 0.6 * math.exp(-0.3 * li)
    dl = diff_lambda.astype(jnp.float32)
    lam = jnp.exp(jnp.sum(dl[0] * dl[1])) - jnp.exp(jnp.sum(dl[2] * dl[3])) + lam_init
    o_a = _diff_attention(q, k, v, lam)
    o_a = (_rms_norm(o_a, diff_subln) * (1.0 - lam_init)).reshape(B, S, BRANCH_WIDTH)

    o_b = b_b * _dwconv(b_c * b_x, sconv_w)

    glu_a, glu_b = jnp.split(c_glu, 2, axis=-1)
    u = _dwconv(glu_a * jax.nn.sigmoid(glu_b), conf_dw_w) + conf_dw_b
    o_c = jax.nn.silu(_layer_norm(u, conf_ln_g, conf_ln_b))

    qd = (_rms_norm(d_cq, mla_q_norm) @ mla_w_uq).reshape(B, S, MLA_HEADS, MLA_NOPE + MLA_ROPE)
    q_nope, q_rot = qd[..., :MLA_NOPE], _rope(qd[..., MLA_NOPE:], MLA_ROPE, MLA_ROPE_THETA)
    kvd = (_rms_norm(d_ckv, mla_kv_norm) @ mla_w_ukv).reshape(B, S, MLA_HEADS, MLA_NOPE + MLA_V)
    k_nope, v_d = kvd[..., :MLA_NOPE], kvd[..., MLA_NOPE:]
    k_rot = _rope(d_kr[:, :, None, :], MLA_ROPE, MLA_ROPE_THETA)
    q_d = jnp.concatenate([q_nope, q_rot], axis=-1)
    k_d = jnp.concatenate([k_nope, jnp.broadcast_to(k_rot, (B, S, MLA_HEADS, MLA_ROPE))], axis=-1)
    o_d = _softmax_attention(q_d, k_d, v_d).reshape(B, S, BRANCH_WIDTH)

    kv_m = (_rms_norm(mem, mem_norm) @ mem_w_kv).reshape(mem.shape[0], N_MEM, 2, MEM_HEADS, MEM_HEAD_DIM)
    q_e = e_q.reshape(B, S, MEM_HEADS, MEM_HEAD_DIM)
    s_e = jnp.einsum("bqhd,bkhd->bhqk", q_e, kv_m[:, :, 0]).astype(jnp.float32) * (MEM_HEAD_DIM ** -0.5)
    p_e = jax.nn.softmax(s_e, axis=-1)
    o_e = jnp.einsum("bhqk,bkhd->bqhd", p_e.astype(x.dtype), kv_m[:, :, 1]).reshape(B, S, BRANCH_WIDTH)

    outs = (o_a * jax.nn.silu(a_z), o_b * jax.nn.silu(b_z), o_c * jax.nn.silu(c_z),
            o_d * jax.nn.silu(d_z), o_e * jax.nn.silu(e_z))
    g = jax.nn.sigmoid(gates.reshape(B, S, N_BRANCHES, D_MODEL))
    y = g[:, :, 0] * (outs[0] @ w_branch[0])
    for i in range(1, N_BRANCHES):
        y = y + g[:, :, i] * (outs[i] @ w_branch[i])
    y = y @ w_out
    return x + _rms_norm(y, norm_post)


def setup_inputs(seed: int = 0) -> dict:
    key = jax.random.key(seed)
    ks = jax.random.split(key, 32)
    f32 = jnp.float32
    n = lambda k, shape, s=1.0: (jax.random.normal(k, shape, f32) * s)
    gain = lambda k, shape: 1.0 + 0.02 * jax.random.normal(k, shape, f32)
    return {
        "x_prompt": n(ks[0], (BATCH, SEQ, D_MODEL)),
        "x_sample": n(ks[1], (DEC_BATCH, DEC_SEQ, D_MODEL)),
        "mem_prompt": n(ks[2], (BATCH, N_MEM, D_MODEL)),
        "mem_sample": n(ks[3], (DEC_BATCH, N_MEM, D_MODEL)),
        "norm_pre": gain(ks[4], (DEPTH, D_MODEL)),
        "w_in": n(ks[5], (DEPTH, D_MODEL, IN_COLS), D_MODEL ** -0.5),
        "diff_lambda": n(ks[6], (DEPTH, 4, DIFF_HEAD_DIM), 0.1),
        "diff_subln": gain(ks[7], (DEPTH, DIFF_V_DIM)),
        "sconv_w": n(ks[8], (DEPTH, SC_KERNEL, BRANCH_WIDTH), SC_KERNEL ** -0.5),
        "conf_dw_w": n(ks[9], (DEPTH, CONF_KERNEL, BRANCH_WIDTH), CONF_KERNEL ** -0.5),
        "conf_dw_b": n(ks[10], (DEPTH, BRANCH_WIDTH), 0.02),
        "conf_ln_g": gain(ks[11], (DEPTH, BRANCH_WIDTH)),
        "conf_ln_b": n(ks[12], (DEPTH, BRANCH_WIDTH), 0.02),
        "mla_q_norm": gain(ks[13], (DEPTH, MLA_Q_RANK)),
        "mla_w_uq": n(ks[14], (DEPTH, MLA_Q_RANK, MLA_HEADS * (MLA_NOPE + MLA_ROPE)), MLA_Q_RANK ** -0.5),
        "mla_kv_norm": gain(ks[15], (DEPTH, MLA_KV_RANK)),
        "mla_w_ukv": n(ks[16], (DEPTH, MLA_KV_RANK, MLA_HEADS * (MLA_NOPE + MLA_V)), MLA_KV_RANK ** -0.5),
        "mem_norm": gain(ks[17], (DEPTH, D_MODEL)),
        "mem_w_kv": n(ks[18], (DEPTH, D_MODEL, 2 * MEM_HEADS * MEM_HEAD_DIM), D_MODEL ** -0.5),
        "w_branch": n(ks[19], (DEPTH, N_BRANCHES, BRANCH_WIDTH, D_MODEL), BRANCH_WIDTH ** -0.5),
        "w_out": n(ks[20], (DEPTH, D_MODEL, D_MODEL), D_MODEL ** -0.5),
        "norm_post": gain(ks[21], (DEPTH, D_MODEL)),
    }


def reference(x_prompt, x_sample, mem_prompt, mem_sample, norm_pre, w_in, diff_lambda,
              diff_subln, sconv_w, conf_dw_w, conf_dw_b, conf_ln_g, conf_ln_b,
              mla_q_norm, mla_w_uq, mla_kv_norm, mla_w_ukv, mem_norm, mem_w_kv,
              w_branch, w_out, norm_post):
    weights = (norm_pre, w_in, diff_lambda, diff_subln, sconv_w, conf_dw_w, conf_dw_b,
               conf_ln_g, conf_ln_b, mla_q_norm, mla_w_uq, mla_kv_norm, mla_w_ukv,
               mem_norm, mem_w_kv, w_branch, w_out, norm_post)
    y_prompt = x_prompt
    y_sample = x_sample
    for li in range(DEPTH):
        layer_w = [w[li] for w in weights]
        y_prompt = _layer(y_prompt, mem_prompt, li, *layer_w)
        y_sample = _layer(y_sample, mem_sample, li, *layer_w)
    return (y_prompt, y_sample)
```

```python
import functools
import math

import jax
import jax.numpy as jnp
from jax import lax
from jax.experimental import pallas as pl
from jax.experimental.pallas import tpu as pltpu

F32 = jnp.float32
BF16 = jnp.bfloat16

D_MODEL = 1024
BRANCH_WIDTH = 512
N_BRANCHES = 5
DIFF_HEADS = 4
DIFF_HEAD_DIM = 64
DIFF_V_DIM = 128
PARTIAL_ROPE_DIM = DIFF_HEAD_DIM // 4
ROPE_THETA = 500000.0
SC_KERNEL = 3
CONF_KERNEL = 31
MLA_HEADS = 4
MLA_NOPE = 64
MLA_ROPE = 32
MLA_V = 128
MLA_Q_RANK = 256
MLA_KV_RANK = 128
MLA_ROPE_THETA = 10000.0
MEM_HEADS = 4
MEM_HEAD_DIM = 128
N_MEM = 256
EPS = 1e-6

LANES = 128
CHUNK = 512
N_GATE_CHUNKS = N_BRANCHES * D_MODEL // CHUNK
C_AQ, C_AK, C_AV, C_AZ = 10, 11, 12, 13
C_BB, C_BC, C_BX, C_BZ = 14, 15, 16, 17
C_GA, C_GB, C_CZ = 18, 19, 20
C_DMIX, C_DZ, C_EQ, C_EZ = 21, 22, 23, 24
N_CHUNKS = 25
PROJ_COLS = N_CHUNKS * CHUNK
SIGMOID_CHUNKS = tuple(range(N_GATE_CHUNKS)) + (C_GB,)
SILU_CHUNKS = (C_AZ, C_BZ, C_CZ, C_DZ, C_EZ)
PLAIN_CHUNKS = (C_AV, C_BB, C_BC, C_BX, C_GA)
MLA_ROT_LANE = MLA_NOPE

VMEM_LIMIT = 56 * 1024 * 1024


def _cparams(sem):
    return pltpu.CompilerParams(dimension_semantics=sem, vmem_limit_bytes=VMEM_LIMIT)


def _is_any(j, values):
    c = j == values[0]
    for v in values[1:]:
        c = jnp.logical_or(c, j == v)
    return c


def _rope_rows(a, tab_ref, shift):
    up = pltpu.roll(a, LANES - shift, 1)
    dn = pltpu.roll(a, shift, 1)
    return a * tab_ref[0] + up * tab_ref[1] + dn * tab_ref[2]


def _inproj_kernel(x_ref, g_ref, w_ref, ra_ref, rd_ref, qn_ref, kvn_ref, o_ref, h_ref):
    j = pl.program_id(1)

    @pl.when(j == 0)
    def _():
        x = x_ref[...]
        ms = jnp.mean(x * x, axis=-1, keepdims=True)
        h_ref[...] = (x * lax.rsqrt(ms + EPS) * g_ref[...]).astype(BF16)

    acc = jnp.dot(h_ref[...], w_ref[...], preferred_element_type=F32)

    @pl.when(_is_any(j, SIGMOID_CHUNKS))
    def _():
        o_ref[...] = jax.nn.sigmoid(acc).astype(BF16)

    @pl.when(_is_any(j, SILU_CHUNKS))
    def _():
        o_ref[...] = (acc * jax.nn.sigmoid(acc)).astype(BF16)

    @pl.when(_is_any(j, PLAIN_CHUNKS))
    def _():
        o_ref[...] = acc.astype(BF16)

    @pl.when(j == C_AQ)
    def _():
        scale = DIFF_HEAD_DIM ** -0.5
        for c in range(CHUNK // LANES):
            sl = slice(c * LANES, (c + 1) * LANES)
            o_ref[:, sl] = (_rope_rows(acc[:, sl], ra_ref, PARTIAL_ROPE_DIM // 2) * scale).astype(BF16)

    @pl.when(j == C_AK)
    def _():
        for c in range(CHUNK // LANES):
            sl = slice(c * LANES, (c + 1) * LANES)
            o_ref[:, sl] = _rope_rows(acc[:, sl], ra_ref, PARTIAL_ROPE_DIM // 2).astype(BF16)

    @pl.when(j == C_DMIX)
    def _():
        cq = acc[:, :MLA_Q_RANK]
        ms = jnp.mean(cq * cq, axis=-1, keepdims=True)
        o_ref[:, :MLA_Q_RANK] = (cq * lax.rsqrt(ms + EPS) * qn_ref[...]).astype(BF16)
        ckv = acc[:, MLA_Q_RANK:MLA_Q_RANK + MLA_KV_RANK]
        ms = jnp.mean(ckv * ckv, axis=-1, keepdims=True)
        o_ref[:, MLA_Q_RANK:MLA_Q_RANK + MLA_KV_RANK] = (ckv * lax.rsqrt(ms + EPS) * kvn_ref[...]).astype(BF16)
        kr = acc[:, MLA_Q_RANK + MLA_KV_RANK:]
        o_ref[:, MLA_Q_RANK + MLA_KV_RANK:] = _rope_rows(kr, rd_ref, MLA_ROPE // 2).astype(BF16)

    @pl.when(j == C_EQ)
    def _():
        o_ref[...] = (acc * (MEM_HEAD_DIM ** -0.5)).astype(BF16)


def _in_proj(x2d, seq, norm_pre, w_perm, rope_a, rope_d, q_norm, kv_norm):
    n = x2d.shape[0]
    tm = min(1024, seq)
    nt = seq // tm
    return pl.pallas_call(
        _inproj_kernel,
        grid=(n // tm, N_CHUNKS),
        in_specs=[
            pl.BlockSpec((tm, D_MODEL), lambda i, j: (i, 0)),
            pl.BlockSpec((1, D_MODEL), lambda i, j: (0, 0)),
            pl.BlockSpec((D_MODEL, CHUNK), lambda i, j: (0, j)),
            pl.BlockSpec((3, tm, LANES), lambda i, j: (0, i % nt, 0)),
            pl.BlockSpec((3, tm, LANES), lambda i, j: (0, i % nt, 0)),
            pl.BlockSpec((1, MLA_Q_RANK), lambda i, j: (0, 0)),
            pl.BlockSpec((1, MLA_KV_RANK), lambda i, j: (0, 0)),
        ],
        out_specs=pl.BlockSpec((tm, CHUNK), lambda i, j: (i, j)),
        out_shape=jax.ShapeDtypeStruct((n, PROJ_COLS), BF16),
        scratch_shapes=[pltpu.VMEM((tm, D_MODEL), BF16)],
        compiler_params=_cparams(("parallel", "arbitrary")),
        name="in_proj",
    )(x2d, norm_pre, w_perm, rope_a, rope_d, q_norm, kv_norm)


def _softmax_parts(s):
    m = jnp.max(s, axis=-1, keepdims=True)
    e = jnp.exp(s - m)
    r = 1.0 / jnp.sum(e, axis=-1, keepdims=True)
    return e, r


_NT = (((1,), (1,)), ((), ()))


def _diff_attn_kernel(lam_init, q_ref, k_ref, v_ref, z_ref, dl_ref, g_ref, o_ref):
    tq = q_ref.shape[1]
    dl = dl_ref[...]
    lam = (jnp.exp(jnp.sum(dl[0:1] * dl[1:2], axis=-1, keepdims=True))
           - jnp.exp(jnp.sum(dl[2:3] * dl[3:4], axis=-1, keepdims=True)) + lam_init)
    q = q_ref[0]
    lane = lax.broadcasted_iota(jnp.int32, q.shape, 1)
    zero = jnp.zeros_like(q)
    q2 = jnp.concatenate([jnp.where(lane < DIFF_HEAD_DIM, q, zero),
                          jnp.where(lane >= DIFF_HEAD_DIM, q, zero)], axis=0)
    s = lax.dot_general(q2, k_ref[0], _NT, preferred_element_type=F32)
    e, r = _softmax_parts(s)
    a = e[:tq] * r[:tq] - e[tq:] * (r[tq:] * lam)
    o = jnp.dot(a.astype(BF16), v_ref[0], preferred_element_type=F32)
    ms = jnp.mean(o * o, axis=-1, keepdims=True)
    on = o * lax.rsqrt(ms + EPS) * g_ref[...]
    o_ref[0] = (on * (1.0 - lam_init) * z_ref[0].astype(F32)).astype(BF16)


def _diff_attn(proj3, diff_lambda, diff_subln, lam_init, tq):
    b, s, _ = proj3.shape
    u = CHUNK // LANES
    return pl.pallas_call(
        functools.partial(_diff_attn_kernel, lam_init),
        grid=(b, DIFF_HEADS, s // tq),
        in_specs=[
            pl.BlockSpec((1, tq, LANES), lambda bi, h, qi: (bi, qi, C_AQ * u + h)),
            pl.BlockSpec((1, s, LANES), lambda bi, h, qi: (bi, 0, C_AK * u + h)),
            pl.BlockSpec((1, s, LANES), lambda bi, h, qi: (bi, 0, C_AV * u + h)),
            pl.BlockSpec((1, tq, LANES), lambda bi, h, qi: (bi, qi, C_AZ * u + h)),
            pl.BlockSpec((4, DIFF_HEAD_DIM), lambda bi, h, qi: (0, 0)),
            pl.BlockSpec((1, DIFF_V_DIM), lambda bi, h, qi: (0, 0)),
        ],
        out_specs=pl.BlockSpec((1, tq, LANES), lambda bi, h, qi: (bi, qi, h)),
        out_shape=jax.ShapeDtypeStruct((b, s, BRANCH_WIDTH), BF16),
        compiler_params=_cparams(("parallel", "parallel", "parallel")),
        name="diff_attn",
    )(proj3, proj3, proj3, proj3, diff_lambda, diff_subln)


def _mla_attn_kernel(cq_ref, ckv_ref, kr_ref, z_ref, wq_ref, wk_ref, wv_ref, rd_ref, o_ref, k_sc, v_sc):
    @pl.when(pl.program_id(2) == 0)
    def _():
        ckv = ckv_ref[0]
        kn = jnp.dot(ckv, wk_ref[0], preferred_element_type=F32)
        k_sc[...] = (kn + kr_ref[0].astype(F32)).astype(BF16)
        v_sc[...] = jnp.dot(ckv, wv_ref[0], preferred_element_type=F32).astype(BF16)

    q = jnp.dot(cq_ref[0], wq_ref[0], preferred_element_type=F32)
    q = _rope_rows(q, rd_ref, MLA_ROPE // 2) * ((MLA_NOPE + MLA_ROPE) ** -0.5)
    s = lax.dot_general(q.astype(BF16), k_sc[...], _NT, preferred_element_type=F32)
    e, r = _softmax_parts(s)
    o = jnp.dot((e * r).astype(BF16), v_sc[...], preferred_element_type=F32)
    o_ref[0] = (o * z_ref[0].astype(F32)).astype(BF16)


def _mla_attn(proj3, wq, wk, wv, rope_d, tq):
    b, s, _ = proj3.shape
    u = CHUNK // LANES
    return pl.pallas_call(
        _mla_attn_kernel,
        grid=(b, MLA_HEADS, s // tq),
        in_specs=[
            pl.BlockSpec((1, tq, MLA_Q_RANK), lambda bi, h, qi: (bi, qi, C_DMIX * CHUNK // MLA_Q_RANK)),
            pl.BlockSpec((1, s, LANES), lambda bi, h, qi: (bi, 0, C_DMIX * u + 2)),
            pl.BlockSpec((1, s, LANES), lambda bi, h, qi: (bi, 0, C_DMIX * u + 3)),
            pl.BlockSpec((1, tq, LANES), lambda bi, h, qi: (bi, qi, C_DZ * u + h)),
            pl.BlockSpec((1, MLA_Q_RANK, LANES), lambda bi, h, qi: (h, 0, 0)),
            pl.BlockSpec((1, MLA_KV_RANK, LANES), lambda bi, h, qi: (h, 0, 0)),
            pl.BlockSpec((1, MLA_KV_RANK, LANES), lambda bi, h, qi: (h, 0, 0)),
            pl.BlockSpec((3, tq, LANES), lambda bi, h, qi: (0, qi, 0)),
        ],
        out_specs=pl.BlockSpec((1, tq, LANES), lambda bi, h, qi: (bi, qi, h)),
        out_shape=jax.ShapeDtypeStruct((b, s, BRANCH_WIDTH), BF16),
        scratch_shapes=[pltpu.VMEM((s, LANES), BF16), pltpu.VMEM((s, LANES), BF16)],
        compiler_params=_cparams(("parallel", "parallel", "arbitrary")),
        name="mla_attn",
    )(proj3, proj3, proj3, proj3, wq, wk, wv, rope_d)


def _mem_attn_kernel(mem_ref, g_ref, w_ref, q_ref, z_ref, o_ref, kv_sc):
    @pl.when(pl.program_id(1) == 0)
    def _():
        m = mem_ref[0]
        ms = jnp.mean(m * m, axis=-1, keepdims=True)
        mn = (m * lax.rsqrt(ms + EPS) * g_ref[...]).astype(BF16)
        kv_sc[...] = jnp.dot(mn, w_ref[...], preferred_element_type=F32).astype(BF16)

    kv_off = MEM_HEADS * MEM_HEAD_DIM
    for h in range(MEM_HEADS):
        sl = slice(h * MEM_HEAD_DIM, (h + 1) * MEM_HEAD_DIM)
        s = lax.dot_general(q_ref[0, :, sl], kv_sc[:, sl], _NT, preferred_element_type=F32)
        e, r = _softmax_parts(s)
        vs = slice(kv_off + h * MEM_HEAD_DIM, kv_off + (h + 1) * MEM_HEAD_DIM)
        o = jnp.dot((e * r).astype(BF16), kv_sc[:, vs], preferred_element_type=F32)
        o_ref[0, :, sl] = (o * z_ref[0, :, sl].astype(F32)).astype(BF16)


def _mem_attn(proj3, mem, mem_norm, w_kv, tq):
    b, s, _ = proj3.shape
    return pl.pallas_call(
        _mem_attn_kernel,
        grid=(b, s // tq),
        in_specs=[
            pl.BlockSpec((1, N_MEM, D_MODEL), lambda bi, qi: (bi, 0, 0)),
            pl.BlockSpec((1, D_MODEL), lambda bi, qi: (0, 0)),
            pl.BlockSpec((D_MODEL, 2 * MEM_HEADS * MEM_HEAD_DIM), lambda bi, qi: (0, 0)),
            pl.BlockSpec((1, tq, CHUNK), lambda bi, qi: (bi, qi, C_EQ)),
            pl.BlockSpec((1, tq, CHUNK), lambda bi, qi: (bi, qi, C_EZ)),
        ],
        out_specs=pl.BlockSpec((1, tq, CHUNK), lambda bi, qi: (bi, qi, 0)),
        out_shape=jax.ShapeDtypeStruct((b, s, BRANCH_WIDTH), BF16),
        scratch_shapes=[pltpu.VMEM((N_MEM, 2 * MEM_HEADS * MEM_HEAD_DIM), BF16)],
        compiler_params=_cparams(("parallel", "arbitrary")),
        name="mem_attn",
    )(mem, mem_norm, w_kv, proj3, proj3)


SUBLANES = 8


def _sconv_kernel(b_ref, c_ref, x_ref, z_ref, w_ref, o_ref, pad_ref):
    s = b_ref.shape[1]
    g = c_ref[0].astype(F32) * x_ref[0].astype(F32)
    zeros = jnp.zeros((SUBLANES, LANES), F32)
    pad_ref[0:SUBLANES] = zeros
    pad_ref[SUBLANES:SUBLANES + s] = g
    pad_ref[SUBLANES + s:2 * SUBLANES + s] = zeros
    w = w_ref[...]
    y = (w[0:1] * pad_ref[SUBLANES - 1:SUBLANES - 1 + s] + w[1:2] * g
         + w[2:3] * pad_ref[SUBLANES + 1:SUBLANES + 1 + s])
    o_ref[0] = (b_ref[0].astype(F32) * y * z_ref[0].astype(F32)).astype(BF16)


def _sconv(proj3, sconv_w):
    b, s, _ = proj3.shape
    u = CHUNK // LANES
    return pl.pallas_call(
        _sconv_kernel,
        grid=(b, u),
        in_specs=[
            pl.BlockSpec((1, s, LANES), lambda bi, c: (bi, 0, C_BB * u + c)),
            pl.BlockSpec((1, s, LANES), lambda bi, c: (bi, 0, C_BC * u + c)),
            pl.BlockSpec((1, s, LANES), lambda bi, c: (bi, 0, C_BX * u + c)),
            pl.BlockSpec((1, s, LANES), lambda bi, c: (bi, 0, C_BZ * u + c)),
            pl.BlockSpec((SC_KERNEL, LANES), lambda bi, c: (0, c)),
        ],
        out_specs=pl.BlockSpec((1, s, LANES), lambda bi, c: (bi, 0, c)),
        out_shape=jax.ShapeDtypeStruct((b, s, BRANCH_WIDTH), BF16),
        scratch_shapes=[pltpu.VMEM((s + 2 * SUBLANES, LANES), F32)],
        compiler_params=_cparams(("parallel", "parallel")),
        name="sconv",
    )(proj3, proj3, proj3, proj3, sconv_w)


CONF_HALO = 16
CONF_ROWS = 64


def _conf_kernel(ga_ref, gb_ref, z_ref, w_ref, b_ref, lg_ref, lb_ref, o_ref, pad_ref):
    s = ga_ref.shape[1]
    zeros = jnp.zeros((CONF_HALO, BRANCH_WIDTH), F32)
    pad_ref[0:CONF_HALO] = zeros
    pad_ref[CONF_HALO:CONF_HALO + s] = ga_ref[0].astype(F32) * gb_ref[0].astype(F32)
    pad_ref[CONF_HALO + s:2 * CONF_HALO + s] = zeros
    first = CONF_HALO - CONF_KERNEL // 2
    span = CONF_ROWS + 2 * CONF_HALO - SUBLANES

    def step(i, carry):
        r0 = pl.multiple_of(i * CONF_ROWS, CONF_ROWS)
        win = pad_ref[pl.ds(r0, CONF_ROWS + 2 * CONF_HALO), :]
        acc = jnp.zeros((CONF_ROWS, BRANCH_WIDTH), F32) + b_ref[...]
        for r in range(SUBLANES):
            shifted = win[r:r + span]
            for k in range(CONF_KERNEL):
                t = first + k
                if t % SUBLANES == r:
                    m = t - r
                    acc = acc + w_ref[k:k + 1, :] * shifted[m:m + CONF_ROWS]
        mu = jnp.mean(acc, axis=-1, keepdims=True)
        d = acc - mu
        var = jnp.mean(d * d, axis=-1, keepdims=True)
        yn = d * lax.rsqrt(var + EPS) * lg_ref[...] + lb_ref[...]
        oc = yn * jax.nn.sigmoid(yn)
        o_ref[0, pl.ds(r0, CONF_ROWS), :] = (oc * z_ref[0, pl.ds(r0, CONF_ROWS), :].astype(F32)).astype(BF16)
        return carry

    lax.fori_loop(0, s // CONF_ROWS, step, 0)


def _conf(proj3, conf_dw_w, conf_dw_b, conf_ln_g, conf_ln_b):
    b, s, _ = proj3.shape
    vec = pl.BlockSpec((1, BRANCH_WIDTH), lambda bi: (0, 0))
    return pl.pallas_call(
        _conf_kernel,
        grid=(b,),
        in_specs=[
            pl.BlockSpec((1, s, CHUNK), lambda bi: (bi, 0, C_GA)),
            pl.BlockSpec((1, s, CHUNK), lambda bi: (bi, 0, C_GB)),
            pl.BlockSpec((1, s, CHUNK), lambda bi: (bi, 0, C_CZ)),
            pl.BlockSpec((CONF_KERNEL, BRANCH_WIDTH), lambda bi: (0, 0)),
            vec, vec, vec,
        ],
        out_specs=pl.BlockSpec((1, s, CHUNK), lambda bi: (bi, 0, 0)),
        out_shape=jax.ShapeDtypeStruct((b, s, BRANCH_WIDTH), BF16),
        scratch_shapes=[pltpu.VMEM((s + 2 * CONF_HALO, BRANCH_WIDTH), F32)],
        compiler_params=_cparams(("parallel",)),
        name="conf",
    )(proj3, proj3, proj3, conf_dw_w, conf_dw_b, conf_ln_g, conf_ln_b)


def _merge_kernel(oa_ref, ob_ref, oc_ref, od_ref, oe_ref, g0, g1, g2, g3, g4, x_ref, wb_ref, wo_ref, n_ref, y_ref):
    outs = (oa_ref, ob_ref, oc_ref, od_ref, oe_ref)
    gates = (g0, g1, g2, g3, g4)
    y = None
    for i in range(N_BRANCHES):
        t = gates[i][...].astype(F32) * jnp.dot(outs[i][...], wb_ref[i], preferred_element_type=F32)
        y = t if y is None else y + t
    z = jnp.dot(y.astype(BF16), wo_ref[...], preferred_element_type=F32)
    ms = jnp.mean(z * z, axis=-1, keepdims=True)
    y_ref[...] = x_ref[...] + z * lax.rsqrt(ms + EPS) * n_ref[...]


def _merge(outs2d, proj2d, x2d, w_branch, w_out, norm_post, tm):
    n = x2d.shape[0]
    o_spec = pl.BlockSpec((tm, BRANCH_WIDTH), lambda i: (i, 0))
    gate_specs = [pl.BlockSpec((tm, D_MODEL), functools.partial(lambda i, c: (i, c), c=c)) for c in range(N_BRANCHES)]
    resident = dict(pipeline_mode=pl.Buffered(1))
    return pl.pallas_call(
        _merge_kernel,
        grid=(n // tm,),
        in_specs=[o_spec] * N_BRANCHES + gate_specs + [
            pl.BlockSpec((tm, D_MODEL), lambda i: (i, 0)),
            pl.BlockSpec((N_BRANCHES, BRANCH_WIDTH, D_MODEL), lambda i: (0, 0, 0), **resident),
            pl.BlockSpec((D_MODEL, D_MODEL), lambda i: (0, 0), **resident),
            pl.BlockSpec((1, D_MODEL), lambda i: (0, 0)),
        ],
        out_specs=pl.BlockSpec((tm, D_MODEL), lambda i: (i, 0)),
        out_shape=jax.ShapeDtypeStruct((n, D_MODEL), F32),
        compiler_params=_cparams(("parallel",)),
        name="merge",
    )(*outs2d, *([proj2d] * N_BRANCHES), x2d, w_branch, w_out, norm_post)


def _rope_table(seq, rd, theta, period, lane0):
    inv = jnp.float32(theta) ** (-(jnp.arange(0, rd, 2, dtype=F32) / rd))
    ang = jnp.arange(seq, dtype=F32)[:, None] * inv[None, :]
    cos, sin = jnp.cos(ang), jnp.sin(ang)
    half = rd // 2
    pos = (jnp.arange(LANES) % period) - lane0
    idx = jnp.clip(pos % half, 0, half - 1)
    first = (pos >= 0) & (pos < half)
    second = (pos >= half) & (pos < rd)
    cos_l, sin_l = cos[:, idx], sin[:, idx]
    c = jnp.where(first | second, cos_l, 1.0)
    up = jnp.where(first, -sin_l, 0.0)
    dn = jnp.where(second, sin_l, 0.0)
    return jnp.stack([c, up, dn]).astype(F32)


def _prep_layer(w_in, mla_w_uq, mla_w_ukv, mem_w_kv, w_branch, w_out):
    split = [512, 1024, 1536, 2048, 2560, 3072, 3584, 4096, 5120, 5632, 5888, 6016, 6048, 6560, 7072, 7584]
    kr0, kr1, g0 = split[11], split[12], split[15]
    rows = w_in.shape[0]
    w_perm = jnp.concatenate([
        w_in[:, g0:], w_in[:, :kr0],
        jnp.zeros((rows, MLA_ROT_LANE), w_in.dtype), w_in[:, kr0:kr1],
        jnp.zeros((rows, LANES - MLA_ROT_LANE - MLA_ROPE), w_in.dtype),
        w_in[:, kr1:g0]], axis=1).astype(BF16)
    dq = MLA_NOPE + MLA_ROPE
    wq = mla_w_uq.reshape(MLA_Q_RANK, MLA_HEADS, dq).transpose(1, 0, 2)
    wq = jnp.pad(wq, ((0, 0), (0, 0), (0, LANES - dq))).astype(BF16)
    wkv = mla_w_ukv.reshape(MLA_KV_RANK, MLA_HEADS, MLA_NOPE + MLA_V).transpose(1, 0, 2)
    wk = jnp.pad(wkv[:, :, :MLA_NOPE], ((0, 0), (0, 0), (0, LANES - MLA_NOPE))).astype(BF16)
    wv = wkv[:, :, MLA_NOPE:].astype(BF16)
    return w_perm, wq, wk, wv, mem_w_kv.astype(BF16), w_branch.astype(BF16), w_out.astype(BF16)


def _layer(x, mem, li, prep, rope_a, rope_d, norm_pre, diff_lambda, diff_subln, sconv_w, conf_dw_w, conf_dw_b,
           conf_ln_g, conf_ln_b, mla_q_norm, mla_kv_norm, mem_norm, norm_post):
    w_perm, wq, wk, wv, w_kv, w_branch, w_out = prep
    b, s, d = x.shape
    row = lambda v: v.reshape(1, -1)
    x2d = x.reshape(b * s, d)
    proj2d = _in_proj(x2d, s, row(norm_pre), w_perm, rope_a, rope_d, row(mla_q_norm), row(mla_kv_norm))
    proj3 = proj2d.reshape(b, s, PROJ_COLS)
    lam_init = 0.8 - 0.6 * math.exp(-0.3 * li)
    tq = min(256, s)
    o_a = _diff_attn(proj3, diff_lambda, row(diff_subln), lam_init, tq)
    o_b = _sconv(proj3, sconv_w)
    o_c = _conf(proj3, conf_dw_w, row(conf_dw_b), row(conf_ln_g), row(conf_ln_b))
    o_d = _mla_attn(proj3, wq, wk, wv, rope_d, tq)
    o_e = _mem_attn(proj3, mem, row(mem_norm), w_kv, min(512, s))
    outs2d = [o.reshape(b * s, BRANCH_WIDTH) for o in (o_a, o_b, o_c, o_d, o_e)]
    y2d = _merge(outs2d, proj2d, x2d, w_branch, w_out, row(norm_post), min(512, s))
    return y2d.reshape(b, s, d)


def kernel(x_prompt, x_sample, mem_prompt, mem_sample, norm_pre, w_in, diff_lambda, diff_subln, sconv_w, conf_dw_w, conf_dw_b, conf_ln_g, conf_ln_b, mla_q_norm, mla_w_uq, mla_kv_norm, mla_w_ukv, mem_norm, mem_w_kv, w_branch, w_out, norm_post):
    depth = w_in.shape[0]
    y_prompt, y_sample = x_prompt, x_sample
    tables = {}
    for s in {x_prompt.shape[1], x_sample.shape[1]}:
        tables[s] = (_rope_table(s, PARTIAL_ROPE_DIM, ROPE_THETA, DIFF_HEAD_DIM, 0),
                     _rope_table(s, MLA_ROPE, MLA_ROPE_THETA, LANES, MLA_ROT_LANE))
    for li in range(depth):
        prep = _prep_layer(w_in[li], mla_w_uq[li], mla_w_ukv[li], mem_w_kv[li], w_branch[li], w_out[li])
        small = (norm_pre[li], diff_lambda[li], diff_subln[li], sconv_w[li], conf_dw_w[li], conf_dw_b[li],
                 conf_ln_g[li], conf_ln_b[li], mla_q_norm[li], mla_kv_norm[li], mem_norm[li], norm_post[li])
        y_prompt = _layer(y_prompt, mem_prompt, li, prep, *tables[y_prompt.shape[1]], *small)
        y_sample = _layer(y_sample, mem_sample, li, prep, *tables[y_sample.shape[1]], *small)
    return (y_prompt, y_sample)
```

```python
import functools
import math

import jax
import jax.numpy as jnp
from jax import lax
from jax.experimental import pallas as pl
from jax.experimental.pallas import tpu as pltpu

F32 = jnp.float32
BF16 = jnp.bfloat16

D_MODEL = 1024
BRANCH_WIDTH = 512
N_BRANCHES = 5
DIFF_HEADS = 4
DIFF_HEAD_DIM = 64
DIFF_V_DIM = 128
PARTIAL_ROPE_DIM = DIFF_HEAD_DIM // 4
ROPE_THETA = 500000.0
SC_KERNEL = 3
CONF_KERNEL = 31
MLA_HEADS = 4
MLA_NOPE = 64
MLA_ROPE = 32
MLA_V = 128
MLA_Q_RANK = 256
MLA_KV_RANK = 128
MLA_ROPE_THETA = 10000.0
MEM_HEADS = 4
MEM_HEAD_DIM = 128
N_MEM = 256
EPS = 1e-6

LANES = 128
CHUNK = 512
N_GATE_CHUNKS = N_BRANCHES * D_MODEL // CHUNK
C_AQ, C_AK, C_AV, C_AZ = 10, 11, 12, 13
C_BB, C_BC, C_BX, C_BZ = 14, 15, 16, 17
C_GA, C_GB, C_CZ = 18, 19, 20
C_DMIX, C_DZ, C_EQ, C_EZ = 21, 22, 23, 24
N_CHUNKS = 25
PROJ_COLS = N_CHUNKS * CHUNK
SIGMOID_CHUNKS = tuple(range(N_GATE_CHUNKS)) + (C_GB,)
SILU_CHUNKS = (C_AZ, C_BZ, C_CZ, C_DZ, C_EZ)
PLAIN_CHUNKS = (C_AV, C_BB, C_BC, C_BX, C_GA)
MLA_ROT_LANE = MLA_NOPE

VMEM_LIMIT = 56 * 1024 * 1024
ATTN_Q_ROWS = 256
ATTN_HEADS_PER_STEP = 4
LOG2E = math.log2(math.e)


def _cparams(sem):
    return pltpu.CompilerParams(dimension_semantics=sem, vmem_limit_bytes=VMEM_LIMIT)


def _is_any(j, values):
    c = j == values[0]
    for v in values[1:]:
        c = jnp.logical_or(c, j == v)
    return c


def _sigmoid(x):
    return 0.5 * jnp.tanh(0.5 * x) + 0.5


def _rope_rows(a, tab_ref, shift):
    up = pltpu.roll(a, LANES - shift, 1)
    dn = pltpu.roll(a, shift, 1)
    return a * tab_ref[0] + up * tab_ref[1] + dn * tab_ref[2]


def _inproj_kernel(x_ref, g_ref, w_ref, ra_ref, rd_ref, qn_ref, kvn_ref, o_ref, h_ref):
    j = pl.program_id(1)

    @pl.when(j == 0)
    def _():
        x = x_ref[...]
        ms = jnp.mean(x * x, axis=-1, keepdims=True)
        h_ref[...] = (x * lax.rsqrt(ms + EPS) * g_ref[...]).astype(BF16)

    def acc_fn():
        return jnp.dot(h_ref[...], w_ref[...], preferred_element_type=F32)

    @pl.when(_is_any(j, SIGMOID_CHUNKS))
    def _():
        o_ref[...] = _sigmoid(acc_fn()).astype(BF16)

    @pl.when(_is_any(j, SILU_CHUNKS))
    def _():
        acc = acc_fn()
        o_ref[...] = (acc * _sigmoid(acc)).astype(BF16)

    @pl.when(_is_any(j, PLAIN_CHUNKS))
    def _():
        o_ref[...] = acc_fn().astype(BF16)

    @pl.when(j == C_AQ)
    def _():
        acc = acc_fn()
        scale = DIFF_HEAD_DIM ** -0.5 * LOG2E
        for c in range(CHUNK // LANES):
            sl = slice(c * LANES, (c + 1) * LANES)
            o_ref[:, sl] = (_rope_rows(acc[:, sl], ra_ref, PARTIAL_ROPE_DIM // 2) * scale).astype(BF16)

    @pl.when(j == C_AK)
    def _():
        acc = acc_fn()
        for c in range(CHUNK // LANES):
            sl = slice(c * LANES, (c + 1) * LANES)
            o_ref[:, sl] = _rope_rows(acc[:, sl], ra_ref, PARTIAL_ROPE_DIM // 2).astype(BF16)

    @pl.when(j == C_DMIX)
    def _():
        acc = acc_fn()
        cq = acc[:, :MLA_Q_RANK]
        ms = jnp.mean(cq * cq, axis=-1, keepdims=True)
        o_ref[:, :MLA_Q_RANK] = (cq * lax.rsqrt(ms + EPS) * qn_ref[...]).astype(BF16)
        ckv = acc[:, MLA_Q_RANK:MLA_Q_RANK + MLA_KV_RANK]
        ms = jnp.mean(ckv * ckv, axis=-1, keepdims=True)
        o_ref[:, MLA_Q_RANK:MLA_Q_RANK + MLA_KV_RANK] = (ckv * lax.rsqrt(ms + EPS) * kvn_ref[...]).astype(BF16)
        kr = acc[:, MLA_Q_RANK + MLA_KV_RANK:]
        o_ref[:, MLA_Q_RANK + MLA_KV_RANK:] = _rope_rows(kr, rd_ref, MLA_ROPE // 2).astype(BF16)

    @pl.when(j == C_EQ)
    def _():
        o_ref[...] = (acc_fn() * (MEM_HEAD_DIM ** -0.5 * LOG2E)).astype(BF16)


def _in_proj(x2d, seq, norm_pre, w_perm, rope_a, rope_d, q_norm, kv_norm):
    n = x2d.shape[0]
    tm = min(1024, seq)
    nt = seq // tm
    return pl.pallas_call(
        _inproj_kernel,
        grid=(n // tm, N_CHUNKS),
        in_specs=[
            pl.BlockSpec((tm, D_MODEL), lambda i, j: (i, 0)),
            pl.BlockSpec((1, D_MODEL), lambda i, j: (0, 0)),
            pl.BlockSpec((D_MODEL, CHUNK), lambda i, j: (0, j)),
            pl.BlockSpec((3, tm, LANES), lambda i, j: (0, i % nt, 0)),
            pl.BlockSpec((3, tm, LANES), lambda i, j: (0, i % nt, 0)),
            pl.BlockSpec((1, MLA_Q_RANK), lambda i, j: (0, 0)),
            pl.BlockSpec((1, MLA_KV_RANK), lambda i, j: (0, 0)),
        ],
        out_specs=pl.BlockSpec((tm, CHUNK), lambda i, j: (i, j)),
        out_shape=jax.ShapeDtypeStruct((n, PROJ_COLS), BF16),
        scratch_shapes=[pltpu.VMEM((tm, D_MODEL), BF16)],
        compiler_params=_cparams(("parallel", "arbitrary")),
        name="in_proj",
    )(x2d, norm_pre, w_perm, rope_a, rope_d, q_norm, kv_norm)


def _softmax_parts(s):
    m = jnp.max(s, axis=-1, keepdims=True)
    e = jnp.exp2(s - m)
    r = 1.0 / jnp.sum(e, axis=-1, keepdims=True)
    return e, r


_NT = (((1,), (1,)), ((), ()))


def _diff_attn_kernel(lam_init, q_ref, k_ref, v_ref, z_ref, dl_ref, g_ref, o_ref):
    tq = q_ref.shape[1]
    dl = dl_ref[...]
    lam = (jnp.exp(jnp.sum(dl[0:1] * dl[1:2], axis=-1, keepdims=True))
           - jnp.exp(jnp.sum(dl[2:3] * dl[3:4], axis=-1, keepdims=True)) + lam_init)
    lane = lax.broadcasted_iota(jnp.int32, (tq, LANES), 1)
    zero = jnp.zeros((tq, LANES), BF16)
    def scores(h):
        sl = slice(h * LANES, (h + 1) * LANES)
        q = q_ref[0, :, sl]
        q2 = jnp.concatenate([jnp.where(lane < DIFF_HEAD_DIM, q, zero),
                              jnp.where(lane >= DIFF_HEAD_DIM, q, zero)], axis=0)
        return lax.dot_general(q2, k_ref[0, :, sl], _NT, preferred_element_type=F32)

    def finish(h, s):
        sl = slice(h * LANES, (h + 1) * LANES)
        e, r = _softmax_parts(s)
        a = e[:tq] * r[:tq] - e[tq:] * (r[tq:] * lam)
        o = jnp.dot(a.astype(BF16), v_ref[0, :, sl], preferred_element_type=F32)
        ms = jnp.mean(o * o, axis=-1, keepdims=True)
        on = o * lax.rsqrt(ms + EPS) * g_ref[...]
        o_ref[0, :, sl] = (on * (1.0 - lam_init) * z_ref[0, :, sl].astype(F32)).astype(BF16)

    n_heads = q_ref.shape[2] // LANES
    s_next = scores(0)
    for h in range(n_heads):
        s_cur = s_next
        if h + 1 < n_heads:
            s_next = scores(h + 1)
        finish(h, s_cur)


def _diff_attn(proj3, diff_lambda, diff_subln, lam_init, tq, hg):
    b, s, _ = proj3.shape
    w = hg * LANES
    u = CHUNK // w
    return pl.pallas_call(
        functools.partial(_diff_attn_kernel, lam_init),
        grid=(b, DIFF_HEADS // hg, s // tq),
        in_specs=[
            pl.BlockSpec((1, tq, w), lambda bi, g, qi: (bi, qi, C_AQ * u + g)),
            pl.BlockSpec((1, s, w), lambda bi, g, qi: (bi, 0, C_AK * u + g)),
            pl.BlockSpec((1, s, w), lambda bi, g, qi: (bi, 0, C_AV * u + g)),
            pl.BlockSpec((1, tq, w), lambda bi, g, qi: (bi, qi, C_AZ * u + g)),
            pl.BlockSpec((4, DIFF_HEAD_DIM), lambda bi, g, qi: (0, 0)),
            pl.BlockSpec((1, DIFF_V_DIM), lambda bi, g, qi: (0, 0)),
        ],
        out_specs=pl.BlockSpec((1, tq, w), lambda bi, g, qi: (bi, qi, g)),
        out_shape=jax.ShapeDtypeStruct((b, s, BRANCH_WIDTH), BF16),
        compiler_params=_cparams(("parallel", "parallel", "parallel")),
        name="diff_attn",
    )(proj3, proj3, proj3, proj3, diff_lambda, diff_subln)


def _mla_attn_kernel(cq_ref, ckv_ref, kr_ref, z_ref, wq_ref, wk_ref, wv_ref, rd_ref, o_ref, k_sc, v_sc):
    hg = wq_ref.shape[0]

    @pl.when(pl.program_id(2) == 0)
    def _():
        ckv = ckv_ref[0]
        kr = kr_ref[0].astype(F32)
        for h in range(hg):
            sl = slice(h * LANES, (h + 1) * LANES)
            kn = jnp.dot(ckv, wk_ref[h], preferred_element_type=F32)
            k_sc[:, sl] = (kn + kr).astype(BF16)
            v_sc[:, sl] = jnp.dot(ckv, wv_ref[h], preferred_element_type=F32).astype(BF16)

    cq = cq_ref[0]
    scale = (MLA_NOPE + MLA_ROPE) ** -0.5 * LOG2E

    def scores(h):
        sl = slice(h * LANES, (h + 1) * LANES)
        q = jnp.dot(cq, wq_ref[h], preferred_element_type=F32)
        q = _rope_rows(q, rd_ref, MLA_ROPE // 2) * scale
        return lax.dot_general(q.astype(BF16), k_sc[:, sl], _NT, preferred_element_type=F32)

    def finish(h, s):
        sl = slice(h * LANES, (h + 1) * LANES)
        e, r = _softmax_parts(s)
        o = jnp.dot((e * r).astype(BF16), v_sc[:, sl], preferred_element_type=F32)
        o_ref[0, :, sl] = (o * z_ref[0, :, sl].astype(F32)).astype(BF16)

    s_next = scores(0)
    for h in range(hg):
        s_cur = s_next
        if h + 1 < hg:
            s_next = scores(h + 1)
        finish(h, s_cur)


def _mla_attn(proj3, wq, wk, wv, rope_d, tq, hg):
    b, s, _ = proj3.shape
    w = hg * LANES
    u = CHUNK // LANES
    return pl.pallas_call(
        _mla_attn_kernel,
        grid=(b, MLA_HEADS // hg, s // tq),
        in_specs=[
            pl.BlockSpec((1, tq, MLA_Q_RANK), lambda bi, g, qi: (bi, qi, C_DMIX * CHUNK // MLA_Q_RANK)),
            pl.BlockSpec((1, s, LANES), lambda bi, g, qi: (bi, 0, C_DMIX * u + 2)),
            pl.BlockSpec((1, s, LANES), lambda bi, g, qi: (bi, 0, C_DMIX * u + 3)),
            pl.BlockSpec((1, tq, w), lambda bi, g, qi: (bi, qi, C_DZ * (CHUNK // w) + g)),
            pl.BlockSpec((hg, MLA_Q_RANK, LANES), lambda bi, g, qi: (g, 0, 0)),
            pl.BlockSpec((hg, MLA_KV_RANK, LANES), lambda bi, g, qi: (g, 0, 0)),
            pl.BlockSpec((hg, MLA_KV_RANK, LANES), lambda bi, g, qi: (g, 0, 0)),
            pl.BlockSpec((3, tq, LANES), lambda bi, g, qi: (0, qi, 0)),
        ],
        out_specs=pl.BlockSpec((1, tq, w), lambda bi, g, qi: (bi, qi, g)),
        out_shape=jax.ShapeDtypeStruct((b, s, BRANCH_WIDTH), BF16),
        scratch_shapes=[pltpu.VMEM((s, w), BF16), pltpu.VMEM((s, w), BF16)],
        compiler_params=_cparams(("parallel", "parallel", "arbitrary")),
        name="mla_attn",
    )(proj3, proj3, proj3, proj3, wq, wk, wv, rope_d)


def _mem_attn_kernel(mem_ref, g_ref, w_ref, q_ref, z_ref, o_ref, kv_sc):
    @pl.when(pl.program_id(1) == 0)
    def _():
        m = mem_ref[0]
        ms = jnp.mean(m * m, axis=-1, keepdims=True)
        mn = (m * lax.rsqrt(ms + EPS) * g_ref[...]).astype(BF16)
        kv_sc[...] = jnp.dot(mn, w_ref[...], preferred_element_type=F32).astype(BF16)

    kv_off = MEM_HEADS * MEM_HEAD_DIM
    for h in range(MEM_HEADS):
        sl = slice(h * MEM_HEAD_DIM, (h + 1) * MEM_HEAD_DIM)
        s = lax.dot_general(q_ref[0, :, sl], kv_sc[:, sl], _NT, preferred_element_type=F32)
        e, r = _softmax_parts(s)
        vs = slice(kv_off + h * MEM_HEAD_DIM, kv_off + (h + 1) * MEM_HEAD_DIM)
        o = jnp.dot((e * r).astype(BF16), kv_sc[:, vs], preferred_element_type=F32)
        o_ref[0, :, sl] = (o * z_ref[0, :, sl].astype(F32)).astype(BF16)


def _mem_attn(proj3, mem, mem_norm, w_kv, tq):
    b, s, _ = proj3.shape
    return pl.pallas_call(
        _mem_attn_kernel,
        grid=(b, s // tq),
        in_specs=[
            pl.BlockSpec((1, N_MEM, D_MODEL), lambda bi, qi: (bi, 0, 0)),
            pl.BlockSpec((1, D_MODEL), lambda bi, qi: (0, 0)),
            pl.BlockSpec((D_MODEL, 2 * MEM_HEADS * MEM_HEAD_DIM), lambda bi, qi: (0, 0)),
            pl.BlockSpec((1, tq, CHUNK), lambda bi, qi: (bi, qi, C_EQ)),
            pl.BlockSpec((1, tq, CHUNK), lambda bi, qi: (bi, qi, C_EZ)),
        ],
        out_specs=pl.BlockSpec((1, tq, CHUNK), lambda bi, qi: (bi, qi, 0)),
        out_shape=jax.ShapeDtypeStruct((b, s, BRANCH_WIDTH), BF16),
        scratch_shapes=[pltpu.VMEM((N_MEM, 2 * MEM_HEADS * MEM_HEAD_DIM), BF16)],
        compiler_params=_cparams(("parallel", "arbitrary")),
        name="mem_attn",
    )(mem, mem_norm, w_kv, proj3, proj3)


SUBLANES = 8


def _sconv_kernel(b_ref, c_ref, x_ref, z_ref, w_ref, o_ref, pad_ref):
    s = b_ref.shape[1]
    g = c_ref[0].astype(F32) * x_ref[0].astype(F32)
    zeros = jnp.zeros((SUBLANES, LANES), F32)
    pad_ref[0:SUBLANES] = zeros
    pad_ref[SUBLANES:SUBLANES + s] = g
    pad_ref[SUBLANES + s:2 * SUBLANES + s] = zeros
    w = w_ref[...]
    y = (w[0:1] * pad_ref[SUBLANES - 1:SUBLANES - 1 + s] + w[1:2] * g
         + w[2:3] * pad_ref[SUBLANES + 1:SUBLANES + 1 + s])
    o_ref[0] = (b_ref[0].astype(F32) * y * z_ref[0].astype(F32)).astype(BF16)


def _sconv(proj3, sconv_w):
    b, s, _ = proj3.shape
    u = CHUNK // LANES
    return pl.pallas_call(
        _sconv_kernel,
        grid=(b, u),
        in_specs=[
            pl.BlockSpec((1, s, LANES), lambda bi, c: (bi, 0, C_BB * u + c)),
            pl.BlockSpec((1, s, LANES), lambda bi, c: (bi, 0, C_BC * u + c)),
            pl.BlockSpec((1, s, LANES), lambda bi, c: (bi, 0, C_BX * u + c)),
            pl.BlockSpec((1, s, LANES), lambda bi, c: (bi, 0, C_BZ * u + c)),
            pl.BlockSpec((SC_KERNEL, LANES), lambda bi, c: (0, c)),
        ],
        out_specs=pl.BlockSpec((1, s, LANES), lambda bi, c: (bi, 0, c)),
        out_shape=jax.ShapeDtypeStruct((b, s, BRANCH_WIDTH), BF16),
        scratch_shapes=[pltpu.VMEM((s + 2 * SUBLANES, LANES), F32)],
        compiler_params=_cparams(("parallel", "parallel")),
        name="sconv",
    )(proj3, proj3, proj3, proj3, sconv_w)


CONF_HALO = 16
CONF_ROWS = 64


def _conf_kernel(ga_ref, gb_ref, z_ref, w_ref, b_ref, lg_ref, lb_ref, o_ref, pad_ref):
    s = ga_ref.shape[1]
    zeros = jnp.zeros((CONF_HALO, BRANCH_WIDTH), F32)
    pad_ref[0:CONF_HALO] = zeros
    pad_ref[CONF_HALO:CONF_HALO + s] = ga_ref[0].astype(F32) * gb_ref[0].astype(F32)
    pad_ref[CONF_HALO + s:2 * CONF_HALO + s] = zeros
    first = CONF_HALO - CONF_KERNEL // 2
    span = CONF_ROWS + 2 * CONF_HALO - SUBLANES

    def step(i, carry):
        r0 = pl.multiple_of(i * CONF_ROWS, CONF_ROWS)
        win = pad_ref[pl.ds(r0, CONF_ROWS + 2 * CONF_HALO), :]
        acc = jnp.zeros((CONF_ROWS, BRANCH_WIDTH), F32) + b_ref[...]
        for r in range(SUBLANES):
            shifted = win[r:r + span]
            for k in range(CONF_KERNEL):
                t = first + k
                if t % SUBLANES == r:
                    m = t - r
                    acc = acc + w_ref[k:k + 1, :] * shifted[m:m + CONF_ROWS]
        mu = jnp.mean(acc, axis=-1, keepdims=True)
        d = acc - mu
        var = jnp.mean(d * d, axis=-1, keepdims=True)
        yn = d * lax.rsqrt(var + EPS) * lg_ref[...] + lb_ref[...]
        oc = yn * _sigmoid(yn)
        o_ref[0, pl.ds(r0, CONF_ROWS), :] = (oc * z_ref[0, pl.ds(r0, CONF_ROWS), :].astype(F32)).astype(BF16)
        return carry

    lax.fori_loop(0, s // CONF_ROWS, step, 0)


def _conf(proj3, conf_dw_w, conf_dw_b, conf_ln_g, conf_ln_b):
    b, s, _ = proj3.shape
    vec = pl.BlockSpec((1, BRANCH_WIDTH), lambda bi: (0, 0))
    return pl.pallas_call(
        _conf_kernel,
        grid=(b,),
        in_specs=[
            pl.BlockSpec((1, s, CHUNK), lambda bi: (bi, 0, C_GA)),
            pl.BlockSpec((1, s, CHUNK), lambda bi: (bi, 0, C_GB)),
            pl.BlockSpec((1, s, CHUNK), lambda bi: (bi, 0, C_CZ)),
            pl.BlockSpec((CONF_KERNEL, BRANCH_WIDTH), lambda bi: (0, 0)),
            vec, vec, vec,
        ],
        out_specs=pl.BlockSpec((1, s, CHUNK), lambda bi: (bi, 0, 0)),
        out_shape=jax.ShapeDtypeStruct((b, s, BRANCH_WIDTH), BF16),
        scratch_shapes=[pltpu.VMEM((s + 2 * CONF_HALO, BRANCH_WIDTH), F32)],
        compiler_params=_cparams(("parallel",)),
        name="conf",
    )(proj3, proj3, proj3, conf_dw_w, conf_dw_b, conf_ln_g, conf_ln_b)


def _merge_kernel(oa_ref, ob_ref, oc_ref, od_ref, oe_ref, g0, g1, g2, g3, g4, x_ref, wb_ref, wo_ref, n_ref, y_ref):
    outs = (oa_ref, ob_ref, oc_ref, od_ref, oe_ref)
    gates = (g0, g1, g2, g3, g4)
    y = None
    for i in range(N_BRANCHES):
        t = gates[i][...].astype(F32) * jnp.dot(outs[i][...], wb_ref[i], preferred_element_type=F32)
        y = t if y is None else y + t
    z = jnp.dot(y.astype(BF16), wo_ref[...], preferred_element_type=F32)
    ms = jnp.mean(z * z, axis=-1, keepdims=True)
    y_ref[...] = x_ref[...] + z * lax.rsqrt(ms + EPS) * n_ref[...]


def _merge(outs2d, proj2d, x2d, w_branch, w_out, norm_post, tm):
    n = x2d.shape[0]
    o_spec = pl.BlockSpec((tm, BRANCH_WIDTH), lambda i: (i, 0))
    gate_specs = [pl.BlockSpec((tm, D_MODEL), functools.partial(lambda i, c: (i, c), c=c)) for c in range(N_BRANCHES)]
    resident = dict(pipeline_mode=pl.Buffered(1))
    return pl.pallas_call(
        _merge_kernel,
        grid=(n // tm,),
        in_specs=[o_spec] * N_BRANCHES + gate_specs + [
            pl.BlockSpec((tm, D_MODEL), lambda i: (i, 0)),
            pl.BlockSpec((N_BRANCHES, BRANCH_WIDTH, D_MODEL), lambda i: (0, 0, 0), **resident),
            pl.BlockSpec((D_MODEL, D_MODEL), lambda i: (0, 0), **resident),
            pl.BlockSpec((1, D_MODEL), lambda i: (0, 0)),
        ],
        out_specs=pl.BlockSpec((tm, D_MODEL), lambda i: (i, 0)),
        out_shape=jax.ShapeDtypeStruct((n, D_MODEL), F32),
        compiler_params=_cparams(("parallel",)),
        name="merge",
    )(*outs2d, *([proj2d] * N_BRANCHES), x2d, w_branch, w_out, norm_post)


def _rope_table(seq, rd, theta, period, lane0):
    inv = jnp.float32(theta) ** (-(jnp.arange(0, rd, 2, dtype=F32) / rd))
    ang = jnp.arange(seq, dtype=F32)[:, None] * inv[None, :]
    cos, sin = jnp.cos(ang), jnp.sin(ang)
    half = rd // 2
    pos = (jnp.arange(LANES) % period) - lane0
    idx = jnp.clip(pos % half, 0, half - 1)
    first = (pos >= 0) & (pos < half)
    second = (pos >= half) & (pos < rd)
    cos_l, sin_l = cos[:, idx], sin[:, idx]
    c = jnp.where(first | second, cos_l, 1.0)
    up = jnp.where(first, -sin_l, 0.0)
    dn = jnp.where(second, sin_l, 0.0)
    return jnp.stack([c, up, dn]).astype(F32)


def _prep_layer(w_in, mla_w_uq, mla_w_ukv, mem_w_kv, w_branch, w_out):
    split = [512, 1024, 1536, 2048, 2560, 3072, 3584, 4096, 5120, 5632, 5888, 6016, 6048, 6560, 7072, 7584]
    kr0, kr1, g0 = split[11], split[12], split[15]
    rows = w_in.shape[0]
    w_perm = jnp.concatenate([
        w_in[:, g0:], w_in[:, :kr0],
        jnp.zeros((rows, MLA_ROT_LANE), w_in.dtype), w_in[:, kr0:kr1],
        jnp.zeros((rows, LANES - MLA_ROT_LANE - MLA_ROPE), w_in.dtype),
        w_in[:, kr1:g0]], axis=1).astype(BF16)
    dq = MLA_NOPE + MLA_ROPE
    wq = mla_w_uq.reshape(MLA_Q_RANK, MLA_HEADS, dq).transpose(1, 0, 2)
    wq = jnp.pad(wq, ((0, 0), (0, 0), (0, LANES - dq))).astype(BF16)
    wkv = mla_w_ukv.reshape(MLA_KV_RANK, MLA_HEADS, MLA_NOPE + MLA_V).transpose(1, 0, 2)
    wk = jnp.pad(wkv[:, :, :MLA_NOPE], ((0, 0), (0, 0), (0, LANES - MLA_NOPE))).astype(BF16)
    wv = wkv[:, :, MLA_NOPE:].astype(BF16)
    return w_perm, wq, wk, wv, mem_w_kv.astype(BF16), w_branch.astype(BF16), w_out.astype(BF16)


def _layer(x, mem, li, prep, rope_a, rope_d, norm_pre, diff_lambda, diff_subln, sconv_w, conf_dw_w, conf_dw_b,
           conf_ln_g, conf_ln_b, mla_q_norm, mla_kv_norm, mem_norm, norm_post):
    w_perm, wq, wk, wv, w_kv, w_branch, w_out = prep
    b, s, d = x.shape
    row = lambda v: v.reshape(1, -1)
    x2d = x.reshape(b * s, d)
    proj2d = _in_proj(x2d, s, row(norm_pre), w_perm, rope_a, rope_d, row(mla_q_norm), row(mla_kv_norm))
    proj3 = proj2d.reshape(b, s, PROJ_COLS)
    lam_init = 0.8 - 0.6 * math.exp(-0.3 * li)
    tq = min(ATTN_Q_ROWS, s)
    o_a = _diff_attn(proj3, diff_lambda, row(diff_subln), lam_init, tq, ATTN_HEADS_PER_STEP)
    o_b = _sconv(proj3, sconv_w)
    o_c = _conf(proj3, conf_dw_w, row(conf_dw_b), row(conf_ln_g), row(conf_ln_b))
    o_d = _mla_attn(proj3, wq, wk, wv, rope_d, tq, ATTN_HEADS_PER_STEP)
    o_e = _mem_attn(proj3, mem, row(mem_norm), w_kv, min(512, s))
    outs2d = [o.reshape(b * s, BRANCH_WIDTH) for o in (o_a, o_b, o_c, o_d, o_e)]
    y2d = _merge(outs2d, proj2d, x2d, w_branch, w_out, row(norm_post), min(512, s))
    return y2d.reshape(b, s, d)


def kernel(x_prompt, x_sample, mem_prompt, mem_sample, norm_pre, w_in, diff_lambda, diff_subln, sconv_w, conf_dw_w, conf_dw_b, conf_ln_g, conf_ln_b, mla_q_norm, mla_w_uq, mla_kv_norm, mla_w_ukv, mem_norm, mem_w_kv, w_branch, w_out, norm_post):
    depth = w_in.shape[0]
    y_prompt, y_sample = x_prompt, x_sample
    tables = {}
    for s in {x_prompt.shape[1], x_sample.shape[1]}:
        tables[s] = (_rope_table(s, PARTIAL_ROPE_DIM, ROPE_THETA, DIFF_HEAD_DIM, 0),
                     _rope_table(s, MLA_ROPE, MLA_ROPE_THETA, LANES, MLA_ROT_LANE))
    for li in range(depth):
        prep = _prep_layer(w_in[li], mla_w_uq[li], mla_w_ukv[li], mem_w_kv[li], w_branch[li], w_out[li])
        small = (norm_pre[li], diff_lambda[li], diff_subln[li], sconv_w[li], conf_dw_w[li], conf_dw_b[li],
                 conf_ln_g[li], conf_ln_b[li], mla_q_norm[li], mla_kv_norm[li], mem_norm[li], norm_post[li])
        y_prompt = _layer(y_prompt, mem_prompt, li, prep, *tables[y_prompt.shape[1]], *small)
        y_sample = _layer(y_sample, mem_sample, li, prep, *tables[y_sample.shape[1]], *small)
    return (y_prompt, y_sample)
```

```python
import functools
import math

import jax
import jax.numpy as jnp
from jax import lax
from jax.experimental import pallas as pl
from jax.experimental.pallas import tpu as pltpu

F32 = jnp.float32
BF16 = jnp.bfloat16

D_MODEL = 1024
BRANCH_WIDTH = 512
N_BRANCHES = 5
DIFF_HEADS = 4
DIFF_HEAD_DIM = 64
DIFF_V_DIM = 128
PARTIAL_ROPE_DIM = DIFF_HEAD_DIM // 4
ROPE_THETA = 500000.0
SC_KERNEL = 3
CONF_KERNEL = 31
MLA_HEADS = 4
MLA_NOPE = 64
MLA_ROPE = 32
MLA_V = 128
MLA_Q_RANK = 256
MLA_KV_RANK = 128
MLA_ROPE_THETA = 10000.0
MEM_HEADS = 4
MEM_HEAD_DIM = 128
N_MEM = 256
EPS = 1e-6

LANES = 128
SUBLANES = 8
CHUNK = 512
N_GATE_CHUNKS = N_BRANCHES * D_MODEL // CHUNK
C_AQ, C_AK, C_AV, C_AZ = 10, 11, 12, 13
C_BB, C_BC, C_BX, C_BZ = 14, 15, 16, 17
C_GA, C_GB, C_CZ = 18, 19, 20
C_DMIX, C_DZ, C_EQ, C_EZ = 21, 22, 23, 24
N_CHUNKS = 25
SIGMOID_CHUNKS = tuple(range(N_GATE_CHUNKS)) + (C_GB,)
SILU_CHUNKS = (C_AZ, C_BZ, C_CZ, C_DZ, C_EZ)
PLAIN_CHUNKS = (C_AV, C_BB, C_BC, C_BX, C_GA)
MLA_ROT_LANE = MLA_NOPE

VMEM_LIMIT = 56 * 1024 * 1024
IN_PROJ_ROWS = 2048
ATTN_Q_ROWS = 1024
ATTN_HEADS_PER_STEP = 4
MEM_Q_ROWS = 2048
LOG2E = math.log2(math.e)


def _cparams(sem):
    return pltpu.CompilerParams(dimension_semantics=sem, vmem_limit_bytes=VMEM_LIMIT)


def _is_any(j, values):
    c = j == values[0]
    for v in values[1:]:
        c = jnp.logical_or(c, j == v)
    return c


def _sigmoid(x):
    return 0.5 * jnp.tanh(0.5 * x) + 0.5


def _rope_rows(a, tab_ref, shift):
    up = pltpu.roll(a, LANES - shift, 1)
    dn = pltpu.roll(a, shift, 1)
    return a * tab_ref[0] + up * tab_ref[1] + dn * tab_ref[2]


def _inproj_kernel(x_ref, g_ref, w_ref, ra_ref, rd_ref, qn_ref, kvn_ref, o_ref, h_ref):
    j = pl.program_id(1)

    @pl.when(j == 0)
    def _():
        x = x_ref[...]
        ms = jnp.mean(x * x, axis=-1, keepdims=True)
        h_ref[...] = (x * lax.rsqrt(ms + EPS) * g_ref[...]).astype(BF16)

    def acc_fn():
        return jnp.dot(h_ref[...], w_ref[0], preferred_element_type=F32)

    @pl.when(_is_any(j, SIGMOID_CHUNKS))
    def _():
        o_ref[0] = _sigmoid(acc_fn()).astype(BF16)

    @pl.when(_is_any(j, SILU_CHUNKS))
    def _():
        acc = acc_fn()
        o_ref[0] = (acc * _sigmoid(acc)).astype(BF16)

    @pl.when(_is_any(j, PLAIN_CHUNKS))
    def _():
        o_ref[0] = acc_fn().astype(BF16)

    @pl.when(j == C_AQ)
    def _():
        acc = acc_fn()
        scale = DIFF_HEAD_DIM ** -0.5 * LOG2E
        for c in range(CHUNK // LANES):
            sl = slice(c * LANES, (c + 1) * LANES)
            o_ref[0, :, sl] = (_rope_rows(acc[:, sl], ra_ref, PARTIAL_ROPE_DIM // 2) * scale).astype(BF16)

    @pl.when(j == C_AK)
    def _():
        acc = acc_fn()
        for c in range(CHUNK // LANES):
            sl = slice(c * LANES, (c + 1) * LANES)
            o_ref[0, :, sl] = _rope_rows(acc[:, sl], ra_ref, PARTIAL_ROPE_DIM // 2).astype(BF16)

    @pl.when(j == C_DMIX)
    def _():
        acc = acc_fn()
        cq = acc[:, :MLA_Q_RANK]
        ms = jnp.mean(cq * cq, axis=-1, keepdims=True)
        o_ref[0, :, :MLA_Q_RANK] = (cq * lax.rsqrt(ms + EPS) * qn_ref[...]).astype(BF16)
        ckv = acc[:, MLA_Q_RANK:MLA_Q_RANK + MLA_KV_RANK]
        ms = jnp.mean(ckv * ckv, axis=-1, keepdims=True)
        o_ref[0, :, MLA_Q_RANK:MLA_Q_RANK + MLA_KV_RANK] = (ckv * lax.rsqrt(ms + EPS) * kvn_ref[...]).astype(BF16)
        kr = acc[:, MLA_Q_RANK + MLA_KV_RANK:]
        o_ref[0, :, MLA_Q_RANK + MLA_KV_RANK:] = _rope_rows(kr, rd_ref, MLA_ROPE // 2).astype(BF16)

    @pl.when(j == C_EQ)
    def _():
        o_ref[0] = (acc_fn() * (MEM_HEAD_DIM ** -0.5 * LOG2E)).astype(BF16)


def _in_proj(x2d, seq, norm_pre, w_perm, rope_a, rope_d, q_norm, kv_norm):
    n = x2d.shape[0]
    tm = min(IN_PROJ_ROWS, seq)
    nt = seq // tm
    return pl.pallas_call(
        _inproj_kernel,
        grid=(n // tm, N_CHUNKS),
        in_specs=[
            pl.BlockSpec((tm, D_MODEL), lambda i, j: (i, 0)),
            pl.BlockSpec((1, D_MODEL), lambda i, j: (0, 0)),
            pl.BlockSpec((1, D_MODEL, CHUNK), lambda i, j: (j, 0, 0)),
            pl.BlockSpec((3, tm, LANES), lambda i, j: (0, i % nt, 0)),
            pl.BlockSpec((3, tm, LANES), lambda i, j: (0, i % nt, 0)),
            pl.BlockSpec((1, MLA_Q_RANK), lambda i, j: (0, 0)),
            pl.BlockSpec((1, MLA_KV_RANK), lambda i, j: (0, 0)),
        ],
        out_specs=pl.BlockSpec((1, tm, CHUNK), lambda i, j: (j, i, 0)),
        out_shape=jax.ShapeDtypeStruct((N_CHUNKS, n, CHUNK), BF16),
        scratch_shapes=[pltpu.VMEM((tm, D_MODEL), BF16)],
        compiler_params=_cparams(("parallel", "arbitrary")),
        name="in_proj",
    )(x2d, norm_pre, w_perm, rope_a, rope_d, q_norm, kv_norm)


def _softmax_parts(s):
    m = jnp.max(s, axis=-1, keepdims=True)
    e = jnp.exp2(s - m)
    r = 1.0 / jnp.sum(e, axis=-1, keepdims=True)
    return e, r


_NT = (((1,), (1,)), ((), ()))


def _diff_attn_kernel(lam_init, q_ref, k_ref, v_ref, z_ref, dl_ref, g_ref, o_ref):
    tq = q_ref.shape[2]
    dl = dl_ref[...]
    lam = (jnp.exp(jnp.sum(dl[0:1] * dl[1:2], axis=-1, keepdims=True))
           - jnp.exp(jnp.sum(dl[2:3] * dl[3:4], axis=-1, keepdims=True)) + lam_init)
    lane = lax.broadcasted_iota(jnp.int32, (tq, LANES), 1)
    zero = jnp.zeros((tq, LANES), BF16)

    def scores(h):
        sl = slice(h * LANES, (h + 1) * LANES)
        q = q_ref[0, 0, :, sl]
        q2 = jnp.concatenate([jnp.where(lane < DIFF_HEAD_DIM, q, zero),
                              jnp.where(lane >= DIFF_HEAD_DIM, q, zero)], axis=0)
        return lax.dot_general(q2, k_ref[0, 0, :, sl], _NT, preferred_element_type=F32)

    def finish(h, s):
        sl = slice(h * LANES, (h + 1) * LANES)
        e, r = _softmax_parts(s)
        a = e[:tq] * r[:tq] - e[tq:] * (r[tq:] * lam)
        o = jnp.dot(a.astype(BF16), v_ref[0, 0, :, sl], preferred_element_type=F32)
        ms = jnp.mean(o * o, axis=-1, keepdims=True)
        on = o * lax.rsqrt(ms + EPS) * g_ref[...]
        o_ref[0, :, sl] = (on * (1.0 - lam_init) * z_ref[0, 0, :, sl].astype(F32)).astype(BF16)

    n_heads = q_ref.shape[3] // LANES
    s_next = scores(0)
    for h in range(n_heads):
        s_cur = s_next
        if h + 1 < n_heads:
            s_next = scores(h + 1)
        finish(h, s_cur)


def _diff_attn(proj4, diff_lambda, diff_subln, lam_init, tq, hg):
    _, b, s, _ = proj4.shape
    w = hg * LANES
    return pl.pallas_call(
        functools.partial(_diff_attn_kernel, lam_init),
        grid=(b, DIFF_HEADS // hg, s // tq),
        in_specs=[
            pl.BlockSpec((1, 1, tq, w), lambda bi, g, qi: (C_AQ, bi, qi, g)),
            pl.BlockSpec((1, 1, s, w), lambda bi, g, qi: (C_AK, bi, 0, g)),
            pl.BlockSpec((1, 1, s, w), lambda bi, g, qi: (C_AV, bi, 0, g)),
            pl.BlockSpec((1, 1, tq, w), lambda bi, g, qi: (C_AZ, bi, qi, g)),
            pl.BlockSpec((4, DIFF_HEAD_DIM), lambda bi, g, qi: (0, 0)),
            pl.BlockSpec((1, DIFF_V_DIM), lambda bi, g, qi: (0, 0)),
        ],
        out_specs=pl.BlockSpec((1, tq, w), lambda bi, g, qi: (bi, qi, g)),
        out_shape=jax.ShapeDtypeStruct((b, s, BRANCH_WIDTH), BF16),
        compiler_params=_cparams(("parallel", "parallel", "parallel")),
        name="diff_attn",
    )(proj4, proj4, proj4, proj4, diff_lambda, diff_subln)


def _mla_attn_kernel(cq_ref, ckv_ref, kr_ref, z_ref, wq_ref, wk_ref, wv_ref, rd_ref, o_ref, k_sc, v_sc):
    hg = wq_ref.shape[0]

    @pl.when(pl.program_id(2) == 0)
    def _():
        ckv = ckv_ref[0, 0]
        kr = kr_ref[0, 0].astype(F32)
        for h in range(hg):
            sl = slice(h * LANES, (h + 1) * LANES)
            kn = jnp.dot(ckv, wk_ref[h], preferred_element_type=F32)
            k_sc[:, sl] = (kn + kr).astype(BF16)
            v_sc[:, sl] = jnp.dot(ckv, wv_ref[h], preferred_element_type=F32).astype(BF16)

    cq = cq_ref[0, 0]
    scale = (MLA_NOPE + MLA_ROPE) ** -0.5 * LOG2E

    def scores(h):
        sl = slice(h * LANES, (h + 1) * LANES)
        q = jnp.dot(cq, wq_ref[h], preferred_element_type=F32)
        q = _rope_rows(q, rd_ref, MLA_ROPE // 2) * scale
        return lax.dot_general(q.astype(BF16), k_sc[:, sl], _NT, preferred_element_type=F32)

    def finish(h, s):
        sl = slice(h * LANES, (h + 1) * LANES)
        e, r = _softmax_parts(s)
        o = jnp.dot(e.astype(BF16), v_sc[:, sl], preferred_element_type=F32) * r
        o_ref[0, :, sl] = (o * z_ref[0, 0, :, sl].astype(F32)).astype(BF16)

    s_next = scores(0)
    for h in range(hg):
        s_cur = s_next
        if h + 1 < hg:
            s_next = scores(h + 1)
        finish(h, s_cur)


def _mla_attn(proj4, wq, wk, wv, rope_d, tq, hg):
    _, b, s, _ = proj4.shape
    w = hg * LANES
    return pl.pallas_call(
        _mla_attn_kernel,
        grid=(b, MLA_HEADS // hg, s // tq),
        in_specs=[
            pl.BlockSpec((1, 1, tq, MLA_Q_RANK), lambda bi, g, qi: (C_DMIX, bi, qi, 0)),
            pl.BlockSpec((1, 1, s, LANES), lambda bi, g, qi: (C_DMIX, bi, 0, MLA_Q_RANK // LANES)),
            pl.BlockSpec((1, 1, s, LANES), lambda bi, g, qi: (C_DMIX, bi, 0, (MLA_Q_RANK + MLA_KV_RANK) // LANES)),
            pl.BlockSpec((1, 1, tq, w), lambda bi, g, qi: (C_DZ, bi, qi, g)),
            pl.BlockSpec((hg, MLA_Q_RANK, LANES), lambda bi, g, qi: (g, 0, 0)),
            pl.BlockSpec((hg, MLA_KV_RANK, LANES), lambda bi, g, qi: (g, 0, 0)),
            pl.BlockSpec((hg, MLA_KV_RANK, LANES), lambda bi, g, qi: (g, 0, 0)),
            pl.BlockSpec((3, tq, LANES), lambda bi, g, qi: (0, qi, 0)),
        ],
        out_specs=pl.BlockSpec((1, tq, w), lambda bi, g, qi: (bi, qi, g)),
        out_shape=jax.ShapeDtypeStruct((b, s, BRANCH_WIDTH), BF16),
        scratch_shapes=[pltpu.VMEM((s, w), BF16), pltpu.VMEM((s, w), BF16)],
        compiler_params=_cparams(("parallel", "parallel", "arbitrary")),
        name="mla_attn",
    )(proj4, proj4, proj4, proj4, wq, wk, wv, rope_d)


def _mem_attn_kernel(mem_ref, g_ref, w_ref, q_ref, z_ref, o_ref, kv_sc):
    @pl.when(pl.program_id(1) == 0)
    def _():
        m = mem_ref[0]
        ms = jnp.mean(m * m, axis=-1, keepdims=True)
        mn = (m * lax.rsqrt(ms + EPS) * g_ref[...]).astype(BF16)
        kv_sc[...] = jnp.dot(mn, w_ref[...], preferred_element_type=F32).astype(BF16)

    kv_off = MEM_HEADS * MEM_HEAD_DIM

    def scores(h):
        sl = slice(h * MEM_HEAD_DIM, (h + 1) * MEM_HEAD_DIM)
        return lax.dot_general(q_ref[0, 0, :, sl], kv_sc[:, sl], _NT, preferred_element_type=F32)

    def finish(h, s):
        sl = slice(h * MEM_HEAD_DIM, (h + 1) * MEM_HEAD_DIM)
        e, r = _softmax_parts(s)
        vs = slice(kv_off + h * MEM_HEAD_DIM, kv_off + (h + 1) * MEM_HEAD_DIM)
        o = jnp.dot(e.astype(BF16), kv_sc[:, vs], preferred_element_type=F32) * r
        o_ref[0, :, sl] = (o * z_ref[0, 0, :, sl].astype(F32)).astype(BF16)

    s_next = scores(0)
    for h in range(MEM_HEADS):
        s_cur = s_next
        if h + 1 < MEM_HEADS:
            s_next = scores(h + 1)
        finish(h, s_cur)


def _mem_attn(proj4, mem, mem_norm, w_kv, tq):
    _, b, s, _ = proj4.shape
    return pl.pallas_call(
        _mem_attn_kernel,
        grid=(b, s // tq),
        in_specs=[
            pl.BlockSpec((1, N_MEM, D_MODEL), lambda bi, qi: (bi, 0, 0)),
            pl.BlockSpec((1, D_MODEL), lambda bi, qi: (0, 0)),
            pl.BlockSpec((D_MODEL, 2 * MEM_HEADS * MEM_HEAD_DIM), lambda bi, qi: (0, 0)),
            pl.BlockSpec((1, 1, tq, CHUNK), lambda bi, qi: (C_EQ, bi, qi, 0)),
            pl.BlockSpec((1, 1, tq, CHUNK), lambda bi, qi: (C_EZ, bi, qi, 0)),
        ],
        out_specs=pl.BlockSpec((1, tq, CHUNK), lambda bi, qi: (bi, qi, 0)),
        out_shape=jax.ShapeDtypeStruct((b, s, BRANCH_WIDTH), BF16),
        scratch_shapes=[pltpu.VMEM((N_MEM, 2 * MEM_HEADS * MEM_HEAD_DIM), BF16)],
        compiler_params=_cparams(("parallel", "arbitrary")),
        name="mem_attn",
    )(mem, mem_norm, w_kv, proj4, proj4)


def _sconv_kernel(b_ref, c_ref, x_ref, z_ref, w_ref, o_ref, pad_ref):
    s = b_ref.shape[2]
    g = c_ref[0, 0].astype(F32) * x_ref[0, 0].astype(F32)
    zeros = jnp.zeros((SUBLANES, LANES), F32)
    pad_ref[0:SUBLANES] = zeros
    pad_ref[SUBLANES:SUBLANES + s] = g
    pad_ref[SUBLANES + s:2 * SUBLANES + s] = zeros
    w = w_ref[...]
    y = (w[0:1] * pad_ref[SUBLANES - 1:SUBLANES - 1 + s] + w[1:2] * g
         + w[2:3] * pad_ref[SUBLANES + 1:SUBLANES + 1 + s])
    o_ref[0] = (b_ref[0, 0].astype(F32) * y * z_ref[0, 0].astype(F32)).astype(BF16)


def _sconv(proj4, sconv_w):
    _, b, s, _ = proj4.shape
    u = CHUNK // LANES
    return pl.pallas_call(
        _sconv_kernel,
        grid=(b, u),
        in_specs=[
            pl.BlockSpec((1, 1, s, LANES), lambda bi, c: (C_BB, bi, 0, c)),
            pl.BlockSpec((1, 1, s, LANES), lambda bi, c: (C_BC, bi, 0, c)),
            pl.BlockSpec((1, 1, s, LANES), lambda bi, c: (C_BX, bi, 0, c)),
            pl.BlockSpec((1, 1, s, LANES), lambda bi, c: (C_BZ, bi, 0, c)),
            pl.BlockSpec((SC_KERNEL, LANES), lambda bi, c: (0, c)),
        ],
        out_specs=pl.BlockSpec((1, s, LANES), lambda bi, c: (bi, 0, c)),
        out_shape=jax.ShapeDtypeStruct((b, s, BRANCH_WIDTH), BF16),
        scratch_shapes=[pltpu.VMEM((s + 2 * SUBLANES, LANES), F32)],
        compiler_params=_cparams(("parallel", "parallel")),
        name="sconv",
    )(proj4, proj4, proj4, proj4, sconv_w)


CONF_HALO = 16
CONF_ROWS = 64


def _conf_kernel(ga_ref, gb_ref, z_ref, w_ref, b_ref, lg_ref, lb_ref, o_ref, pad_ref):
    s = ga_ref.shape[2]
    zeros = jnp.zeros((CONF_HALO, BRANCH_WIDTH), F32)
    pad_ref[0:CONF_HALO] = zeros
    pad_ref[CONF_HALO:CONF_HALO + s] = ga_ref[0, 0].astype(F32) * gb_ref[0, 0].astype(F32)
    pad_ref[CONF_HALO + s:2 * CONF_HALO + s] = zeros
    first = CONF_HALO - CONF_KERNEL // 2
    rows = CONF_ROWS + 2 * CONF_HALO

    def step(i, carry):
        r0 = pl.multiple_of(i * CONF_ROWS, CONF_ROWS)
        win = pad_ref[pl.ds(r0, rows), :]
        acc = jnp.zeros((CONF_ROWS, BRANCH_WIDTH), F32) + b_ref[...]
        for r in range(SUBLANES):
            shifted = win if r == 0 else pltpu.roll(win, rows - r, 0)
            for k in range(CONF_KERNEL):
                t = first + k
                if t % SUBLANES == r:
                    m = t - r
                    acc = acc + w_ref[k:k + 1, :] * shifted[m:m + CONF_ROWS]
        mu = jnp.mean(acc, axis=-1, keepdims=True)
        d = acc - mu
        var = jnp.mean(d * d, axis=-1, keepdims=True)
        yn = d * lax.rsqrt(var + EPS) * lg_ref[...] + lb_ref[...]
        oc = yn * _sigmoid(yn)
        o_ref[0, pl.ds(r0, CONF_ROWS), :] = (oc * z_ref[0, 0, pl.ds(r0, CONF_ROWS), :].astype(F32)).astype(BF16)
        return carry

    lax.fori_loop(0, s // CONF_ROWS, step, 0)


def _conf(proj4, conf_dw_w, conf_dw_b, conf_ln_g, conf_ln_b):
    _, b, s, _ = proj4.shape
    vec = pl.BlockSpec((1, BRANCH_WIDTH), lambda bi: (0, 0))
    return pl.pallas_call(
        _conf_kernel,
        grid=(b,),
        in_specs=[
            pl.BlockSpec((1, 1, s, CHUNK), lambda bi: (C_GA, bi, 0, 0)),
            pl.BlockSpec((1, 1, s, CHUNK), lambda bi: (C_GB, bi, 0, 0)),
            pl.BlockSpec((1, 1, s, CHUNK), lambda bi: (C_CZ, bi, 0, 0)),
            pl.BlockSpec((CONF_KERNEL, BRANCH_WIDTH), lambda bi: (0, 0)),
            vec, vec, vec,
        ],
        out_specs=pl.BlockSpec((1, s, CHUNK), lambda bi: (bi, 0, 0)),
        out_shape=jax.ShapeDtypeStruct((b, s, BRANCH_WIDTH), BF16),
        scratch_shapes=[pltpu.VMEM((s + 2 * CONF_HALO, BRANCH_WIDTH), F32)],
        compiler_params=_cparams(("parallel",)),
        name="conf",
    )(proj4, proj4, proj4, conf_dw_w, conf_dw_b, conf_ln_g, conf_ln_b)


def _merge_kernel(oa_ref, ob_ref, oc_ref, od_ref, oe_ref, g0, g1, g2, g3, g4, x_ref, wb_ref, wo_ref, n_ref, y_ref):
    outs = (oa_ref, ob_ref, oc_ref, od_ref, oe_ref)
    gates = (g0, g1, g2, g3, g4)
    halves = []
    for c in range(D_MODEL // CHUNK):
        cols = slice(c * CHUNK, (c + 1) * CHUNK)
        y = None
        for i in range(N_BRANCHES):
            t = gates[i][c].astype(F32) * jnp.dot(outs[i][...], wb_ref[i, :, cols], preferred_element_type=F32)
            y = t if y is None else y + t
        halves.append(y.astype(BF16))
    z = jnp.dot(jnp.concatenate(halves, axis=-1), wo_ref[...], preferred_element_type=F32)
    ms = jnp.mean(z * z, axis=-1, keepdims=True)
    y_ref[...] = x_ref[...] + z * lax.rsqrt(ms + EPS) * n_ref[...]


def _merge(outs2d, proj3, x2d, w_branch, w_out, norm_post, tm):
    n = x2d.shape[0]
    o_spec = pl.BlockSpec((tm, BRANCH_WIDTH), lambda i: (i, 0))
    gate_specs = [pl.BlockSpec((D_MODEL // CHUNK, tm, CHUNK), functools.partial(lambda i, c: (c, i, 0), c=c))
                  for c in range(N_BRANCHES)]
    resident = dict(pipeline_mode=pl.Buffered(1))
    return pl.pallas_call(
        _merge_kernel,
        grid=(n // tm,),
        in_specs=[o_spec] * N_BRANCHES + gate_specs + [
            pl.BlockSpec((tm, D_MODEL), lambda i: (i, 0)),
            pl.BlockSpec((N_BRANCHES, BRANCH_WIDTH, D_MODEL), lambda i: (0, 0, 0), **resident),
            pl.BlockSpec((D_MODEL, D_MODEL), lambda i: (0, 0), **resident),
            pl.BlockSpec((1, D_MODEL), lambda i: (0, 0)),
        ],
        out_specs=pl.BlockSpec((tm, D_MODEL), lambda i: (i, 0)),
        out_shape=jax.ShapeDtypeStruct((n, D_MODEL), F32),
        compiler_params=_cparams(("parallel",)),
        name="merge",
    )(*outs2d, *([proj3] * N_BRANCHES), x2d, w_branch, w_out, norm_post)


def _rope_table(seq, rd, theta, period, lane0):
    inv = jnp.float32(theta) ** (-(jnp.arange(0, rd, 2, dtype=F32) / rd))
    ang = jnp.arange(seq, dtype=F32)[:, None] * inv[None, :]
    cos, sin = jnp.cos(ang), jnp.sin(ang)
    half = rd // 2
    pos = (jnp.arange(LANES) % period) - lane0
    idx = jnp.clip(pos % half, 0, half - 1)
    first = (pos >= 0) & (pos < half)
    second = (pos >= half) & (pos < rd)
    cos_l, sin_l = cos[:, idx], sin[:, idx]
    c = jnp.where(first | second, cos_l, 1.0)
    up = jnp.where(first, -sin_l, 0.0)
    dn = jnp.where(second, sin_l, 0.0)
    return jnp.stack([c, up, dn]).astype(F32)


def _prep_layer(w_in, mla_w_uq, mla_w_ukv, mem_w_kv, w_branch, w_out):
    split = [512, 1024, 1536, 2048, 2560, 3072, 3584, 4096, 5120, 5632, 5888, 6016, 6048, 6560, 7072, 7584]
    kr0, kr1, g0 = split[11], split[12], split[15]
    rows = w_in.shape[0]
    w_perm = jnp.concatenate([
        w_in[:, g0:], w_in[:, :kr0],
        jnp.zeros((rows, MLA_ROT_LANE), w_in.dtype), w_in[:, kr0:kr1],
        jnp.zeros((rows, LANES - MLA_ROT_LANE - MLA_ROPE), w_in.dtype),
        w_in[:, kr1:g0]], axis=1).astype(BF16)
    w_perm = w_perm.reshape(rows, N_CHUNKS, CHUNK).transpose(1, 0, 2)
    dq = MLA_NOPE + MLA_ROPE
    wq = mla_w_uq.reshape(MLA_Q_RANK, MLA_HEADS, dq).transpose(1, 0, 2)
    wq = jnp.pad(wq, ((0, 0), (0, 0), (0, LANES - dq))).astype(BF16)
    wkv = mla_w_ukv.reshape(MLA_KV_RANK, MLA_HEADS, MLA_NOPE + MLA_V).transpose(1, 0, 2)
    wk = jnp.pad(wkv[:, :, :MLA_NOPE], ((0, 0), (0, 0), (0, LANES - MLA_NOPE))).astype(BF16)
    wv = wkv[:, :, MLA_NOPE:].astype(BF16)
    return w_perm, wq, wk, wv, mem_w_kv.astype(BF16), w_branch.astype(BF16), w_out.astype(BF16)


def _layer(x, mem, li, prep, rope_a, rope_d, norm_pre, diff_lambda, diff_subln, sconv_w, conf_dw_w, conf_dw_b,
           conf_ln_g, conf_ln_b, mla_q_norm, mla_kv_norm, mem_norm, norm_post):
    w_perm, wq, wk, wv, w_kv, w_branch, w_out = prep
    b, s, d = x.shape
    row = lambda v: v.reshape(1, -1)
    x2d = x.reshape(b * s, d)
    proj3 = _in_proj(x2d, s, row(norm_pre), w_perm, rope_a, rope_d, row(mla_q_norm), row(mla_kv_norm))
    proj4 = proj3.reshape(N_CHUNKS, b, s, CHUNK)
    lam_init = 0.8 - 0.6 * math.exp(-0.3 * li)
    tq = min(ATTN_Q_ROWS, s)
    o_a = _diff_attn(proj4, diff_lambda, row(diff_subln), lam_init, tq, ATTN_HEADS_PER_STEP)
    o_b = _sconv(proj4, sconv_w)
    o_c = _conf(proj4, conf_dw_w, row(conf_dw_b), row(conf_ln_g), row(conf_ln_b))
    o_d = _mla_attn(proj4, wq, wk, wv, rope_d, tq, ATTN_HEADS_PER_STEP)
    o_e = _mem_attn(proj4, mem, row(mem_norm), w_kv, min(MEM_Q_ROWS, s))
    outs2d = [o.reshape(b * s, BRANCH_WIDTH) for o in (o_a, o_b, o_c, o_d, o_e)]
    y2d = _merge(outs2d, proj3, x2d, w_branch, w_out, row(norm_post), min(512, s))
    return y2d.reshape(b, s, d)


def kernel(x_prompt, x_sample, mem_prompt, mem_sample, norm_pre, w_in, diff_lambda, diff_subln, sconv_w, conf_dw_w, conf_dw_b, conf_ln_g, conf_ln_b, mla_q_norm, mla_w_uq, mla_kv_norm, mla_w_ukv, mem_norm, mem_w_kv, w_branch, w_out, norm_post):
    depth = w_in.shape[0]
    y_prompt, y_sample = x_prompt, x_sample
    tables = {}
    for s in {x_prompt.shape[1], x_sample.shape[1]}:
        tables[s] = (_rope_table(s, PARTIAL_ROPE_DIM, ROPE_THETA, DIFF_HEAD_DIM, 0),
                     _rope_table(s, MLA_ROPE, MLA_ROPE_THETA, LANES, MLA_ROT_LANE))
    for li in range(depth):
        prep = _prep_layer(w_in[li], mla_w_uq[li], mla_w_ukv[li], mem_w_kv[li], w_branch[li], w_out[li])
        small = (norm_pre[li], diff_lambda[li], diff_subln[li], sconv_w[li], conf_dw_w[li], conf_dw_b[li],
                 conf_ln_g[li], conf_ln_b[li], mla_q_norm[li], mla_kv_norm[li], mem_norm[li], norm_post[li])
        y_prompt = _layer(y_prompt, mem_prompt, li, prep, *tables[y_prompt.shape[1]], *small)
        y_sample = _layer(y_sample, mem_sample, li, prep, *tables[y_sample.shape[1]], *small)
    return (y_prompt, y_sample)
```

```python
import functools
import math

import jax
import jax.numpy as jnp
from jax import lax
from jax.experimental import pallas as pl
from jax.experimental.pallas import tpu as pltpu

F32 = jnp.float32
BF16 = jnp.bfloat16

D_MODEL = 1024
BRANCH_WIDTH = 512
N_BRANCHES = 5
DIFF_HEADS = 4
DIFF_HEAD_DIM = 64
DIFF_V_DIM = 128
PARTIAL_ROPE_DIM = DIFF_HEAD_DIM // 4
ROPE_THETA = 500000.0
SC_KERNEL = 3
CONF_KERNEL = 31
MLA_HEADS = 4
MLA_NOPE = 64
MLA_ROPE = 32
MLA_V = 128
MLA_Q_RANK = 256
MLA_KV_RANK = 128
MLA_ROPE_THETA = 10000.0
MEM_HEADS = 4
MEM_HEAD_DIM = 128
N_MEM = 256
EPS = 1e-6

LANES = 128
SUBLANES = 8
CHUNK = 512
N_GATE_CHUNKS = N_BRANCHES * D_MODEL // CHUNK
C_AQ, C_AK, C_AV, C_AZ = 10, 11, 12, 13
C_BB, C_BC, C_BX, C_BZ = 14, 15, 16, 17
C_GA, C_GB, C_CZ = 18, 19, 20
C_DMIX, C_DZ, C_EQ, C_EZ = 21, 22, 23, 24
N_CHUNKS = 25
SIGMOID_CHUNKS = tuple(range(N_GATE_CHUNKS)) + (C_GB,)
SILU_CHUNKS = (C_AZ, C_BZ, C_CZ, C_DZ, C_EZ)
PLAIN_CHUNKS = (C_AV, C_BB, C_BC, C_BX, C_GA)
MLA_ROT_LANE = MLA_NOPE

VMEM_LIMIT = 56 * 1024 * 1024
IN_PROJ_ROWS = 1024
ATTN_Q_ROWS = 1024
ATTN_HEADS_PER_STEP = 4
MEM_Q_ROWS = 2048
LOG2E = math.log2(math.e)


def _cparams(sem):
    return pltpu.CompilerParams(dimension_semantics=sem, vmem_limit_bytes=VMEM_LIMIT)


def _is_any(j, values):
    c = j == values[0]
    for v in values[1:]:
        c = jnp.logical_or(c, j == v)
    return c


def _sigmoid(x):
    return 0.5 * jnp.tanh(0.5 * x) + 0.5


def _rope_rows(a, tab_ref, shift):
    up = pltpu.roll(a, LANES - shift, 1)
    dn = pltpu.roll(a, shift, 1)
    return a * tab_ref[0] + up * tab_ref[1] + dn * tab_ref[2]


STEP_CHUNKS = 5
N_STEPS = -(-N_CHUNKS // STEP_CHUNKS)


def _chunk_kind(c):
    if c >= N_CHUNKS:
        return None
    if c in SIGMOID_CHUNKS:
        return "sigmoid"
    if c in SILU_CHUNKS:
        return "silu"
    if c in PLAIN_CHUNKS:
        return "plain"
    return {C_AQ: "rope_q", C_AK: "rope_k", C_DMIX: "dmix", C_EQ: "mem_q"}[c]


def _row_mean_sq(v):
    sq = v * v
    hi = sq.astype(BF16)
    lo = (sq - hi.astype(F32)).astype(BF16)
    ones = jnp.ones((v.shape[1], LANES), BF16)
    tot = jnp.dot(hi, ones, preferred_element_type=F32) + jnp.dot(lo, ones, preferred_element_type=F32)
    return tot * (1.0 / v.shape[1])


def _inproj_kernel(x_ref, g_ref, w_ref, ra_ref, rd_ref, qn_ref, kvn_ref, o_ref, h_ref):
    j = pl.program_id(1)

    @pl.when(j == 0)
    def _():
        x = x_ref[...]
        ms = jnp.mean(x * x, axis=-1, keepdims=True)
        h_ref[...] = (x * lax.rsqrt(ms + EPS) * g_ref[...]).astype(BF16)

    def epilogue(kind, k):
        acc = jnp.dot(h_ref[...], w_ref[k], preferred_element_type=F32)
        if kind == "sigmoid":
            o_ref[k] = _sigmoid(acc).astype(BF16)
        elif kind == "silu":
            o_ref[k] = (acc * _sigmoid(acc)).astype(BF16)
        elif kind == "plain":
            o_ref[k] = acc.astype(BF16)
        elif kind == "mem_q":
            o_ref[k] = (acc * (MEM_HEAD_DIM ** -0.5 * LOG2E)).astype(BF16)
        elif kind in ("rope_q", "rope_k"):
            scale = DIFF_HEAD_DIM ** -0.5 * LOG2E if kind == "rope_q" else 1.0
            for c in range(CHUNK // LANES):
                sl = slice(c * LANES, (c + 1) * LANES)
                o_ref[k, :, sl] = (_rope_rows(acc[:, sl], ra_ref, PARTIAL_ROPE_DIM // 2) * scale).astype(BF16)
        elif kind == "dmix":
            cq = acc[:, :MLA_Q_RANK]
            rs = lax.rsqrt(_row_mean_sq(cq) + EPS)
            for c in range(MLA_Q_RANK // LANES):
                sl = slice(c * LANES, (c + 1) * LANES)
                o_ref[k, :, sl] = (cq[:, sl] * rs * qn_ref[:, sl]).astype(BF16)
            ckv = acc[:, MLA_Q_RANK:MLA_Q_RANK + MLA_KV_RANK]
            rs = lax.rsqrt(_row_mean_sq(ckv) + EPS)
            o_ref[k, :, MLA_Q_RANK:MLA_Q_RANK + MLA_KV_RANK] = (ckv * rs * kvn_ref[...]).astype(BF16)
            kr = acc[:, MLA_Q_RANK + MLA_KV_RANK:]
            o_ref[k, :, MLA_Q_RANK + MLA_KV_RANK:] = _rope_rows(kr, rd_ref, MLA_ROPE // 2).astype(BF16)

    kinds = {}
    for step in range(N_STEPS):
        pair = tuple(_chunk_kind(step * STEP_CHUNKS + k) for k in range(STEP_CHUNKS))
        kinds.setdefault(pair, []).append(step)
    for pair, steps in kinds.items():
        def branch(pair=pair):
            for k, kind in enumerate(pair):
                if kind is not None:
                    epilogue(kind, k)
        pl.when(_is_any(j, steps))(branch)


def _in_proj(x2d, seq, norm_pre, w_perm, rope_a, rope_d, q_norm, kv_norm):
    n = x2d.shape[0]
    tm = min(IN_PROJ_ROWS, seq)
    nt = seq // tm
    return pl.pallas_call(
        _inproj_kernel,
        grid=(n // tm, N_STEPS),
        in_specs=[
            pl.BlockSpec((tm, D_MODEL), lambda i, j: (i, 0)),
            pl.BlockSpec((1, D_MODEL), lambda i, j: (0, 0)),
            pl.BlockSpec((STEP_CHUNKS, D_MODEL, CHUNK), lambda i, j: (j, 0, 0)),
            pl.BlockSpec((3, tm, LANES), lambda i, j: (0, i % nt, 0)),
            pl.BlockSpec((3, tm, LANES), lambda i, j: (0, i % nt, 0)),
            pl.BlockSpec((1, MLA_Q_RANK), lambda i, j: (0, 0)),
            pl.BlockSpec((1, MLA_KV_RANK), lambda i, j: (0, 0)),
        ],
        out_specs=pl.BlockSpec((STEP_CHUNKS, tm, CHUNK), lambda i, j: (j, i, 0)),
        out_shape=jax.ShapeDtypeStruct((N_CHUNKS, n, CHUNK), BF16),
        scratch_shapes=[pltpu.VMEM((tm, D_MODEL), BF16)],
        compiler_params=_cparams(("parallel", "arbitrary")),
        name="in_proj",
    )(x2d, norm_pre, w_perm, rope_a, rope_d, q_norm, kv_norm)


def _softmax_parts(s):
    m = jnp.max(s, axis=-1, keepdims=True)
    e = jnp.exp2(s - m)
    r = 1.0 / jnp.sum(e, axis=-1, keepdims=True)
    return e, r


_NT = (((1,), (1,)), ((), ()))


def _diff_attn_kernel(lam_init, q_ref, k_ref, v_ref, z_ref, dl_ref, g_ref, o_ref):
    tq = q_ref.shape[2]
    dl = dl_ref[...]
    lam = (jnp.exp(jnp.sum(dl[0:1] * dl[1:2], axis=-1, keepdims=True))
           - jnp.exp(jnp.sum(dl[2:3] * dl[3:4], axis=-1, keepdims=True)) + lam_init)
    lane = lax.broadcasted_iota(jnp.int32, (tq, LANES), 1)
    zero = jnp.zeros((tq, LANES), BF16)

    def scores(h):
        sl = slice(h * LANES, (h + 1) * LANES)
        q = q_ref[0, 0, :, sl]
        q2 = jnp.concatenate([jnp.where(lane < DIFF_HEAD_DIM, q, zero),
                              jnp.where(lane >= DIFF_HEAD_DIM, q, zero)], axis=0)
        return lax.dot_general(q2, k_ref[0, 0, :, sl], _NT, preferred_element_type=F32)

    def finish(h, s):
        sl = slice(h * LANES, (h + 1) * LANES)
        e, r = _softmax_parts(s)
        a = e[:tq] * r[:tq] - e[tq:] * (r[tq:] * lam)
        o = jnp.dot(a.astype(BF16), v_ref[0, 0, :, sl], preferred_element_type=F32)
        ms = jnp.mean(o * o, axis=-1, keepdims=True)
        on = o * lax.rsqrt(ms + EPS) * g_ref[...]
        o_ref[0, :, sl] = (on * (1.0 - lam_init) * z_ref[0, 0, :, sl].astype(F32)).astype(BF16)

    n_heads = q_ref.shape[3] // LANES
    s_next = scores(0)
    for h in range(n_heads):
        s_cur = s_next
        if h + 1 < n_heads:
            s_next = scores(h + 1)
        finish(h, s_cur)


def _diff_attn(proj4, diff_lambda, diff_subln, lam_init, tq, hg):
    _, b, s, _ = proj4.shape
    w = hg * LANES
    return pl.pallas_call(
        functools.partial(_diff_attn_kernel, lam_init),
        grid=(b, DIFF_HEADS // hg, s // tq),
        in_specs=[
            pl.BlockSpec((1, 1, tq, w), lambda bi, g, qi: (C_AQ, bi, qi, g)),
            pl.BlockSpec((1, 1, s, w), lambda bi, g, qi: (C_AK, bi, 0, g)),
            pl.BlockSpec((1, 1, s, w), lambda bi, g, qi: (C_AV, bi, 0, g)),
            pl.BlockSpec((1, 1, tq, w), lambda bi, g, qi: (C_AZ, bi, qi, g)),
            pl.BlockSpec((4, DIFF_HEAD_DIM), lambda bi, g, qi: (0, 0)),
            pl.BlockSpec((1, DIFF_V_DIM), lambda bi, g, qi: (0, 0)),
        ],
        out_specs=pl.BlockSpec((1, tq, w), lambda bi, g, qi: (bi, qi, g)),
        out_shape=jax.ShapeDtypeStruct((b, s, BRANCH_WIDTH), BF16),
        compiler_params=_cparams(("parallel", "parallel", "parallel")),
        name="diff_attn",
    )(proj4, proj4, proj4, proj4, diff_lambda, diff_subln)


def _mla_attn_kernel(cq_ref, ckv_ref, kr_ref, z_ref, wq_ref, wk_ref, wv_ref, rd_ref, o_ref, k_sc, v_sc):
    hg = wq_ref.shape[0]

    @pl.when(pl.program_id(2) == 0)
    def _():
        ckv = ckv_ref[0, 0]
        kr = kr_ref[0, 0].astype(F32)
        for h in range(hg):
            sl = slice(h * LANES, (h + 1) * LANES)
            kn = jnp.dot(ckv, wk_ref[h], preferred_element_type=F32)
            k_sc[:, sl] = (kn + kr).astype(BF16)
            v_sc[:, sl] = jnp.dot(ckv, wv_ref[h], preferred_element_type=F32).astype(BF16)

    cq = cq_ref[0, 0]
    scale = (MLA_NOPE + MLA_ROPE) ** -0.5 * LOG2E

    def scores(h):
        sl = slice(h * LANES, (h + 1) * LANES)
        q = jnp.dot(cq, wq_ref[h], preferred_element_type=F32)
        q = _rope_rows(q, rd_ref, MLA_ROPE // 2) * scale
        return lax.dot_general(q.astype(BF16), k_sc[:, sl], _NT, preferred_element_type=F32)

    def finish(h, s):
        sl = slice(h * LANES, (h + 1) * LANES)
        e, r = _softmax_parts(s)
        o = jnp.dot(e.astype(BF16), v_sc[:, sl], preferred_element_type=F32) * r
        o_ref[0, :, sl] = (o * z_ref[0, 0, :, sl].astype(F32)).astype(BF16)

    s_next = scores(0)
    for h in range(hg):
        s_cur = s_next
        if h + 1 < hg:
            s_next = scores(h + 1)
        finish(h, s_cur)


def _mla_attn(proj4, wq, wk, wv, rope_d, tq, hg):
    _, b, s, _ = proj4.shape
    w = hg * LANES
    return pl.pallas_call(
        _mla_attn_kernel,
        grid=(b, MLA_HEADS // hg, s // tq),
        in_specs=[
            pl.BlockSpec((1, 1, tq, MLA_Q_RANK), lambda bi, g, qi: (C_DMIX, bi, qi, 0)),
            pl.BlockSpec((1, 1, s, LANES), lambda bi, g, qi: (C_DMIX, bi, 0, MLA_Q_RANK // LANES)),
            pl.BlockSpec((1, 1, s, LANES), lambda bi, g, qi: (C_DMIX, bi, 0, (MLA_Q_RANK + MLA_KV_RANK) // LANES)),
            pl.BlockSpec((1, 1, tq, w), lambda bi, g, qi: (C_DZ, bi, qi, g)),
            pl.BlockSpec((hg, MLA_Q_RANK, LANES), lambda bi, g, qi: (g, 0, 0)),
            pl.BlockSpec((hg, MLA_KV_RANK, LANES), lambda bi, g, qi: (g, 0, 0)),
            pl.BlockSpec((hg, MLA_KV_RANK, LANES), lambda bi, g, qi: (g, 0, 0)),
            pl.BlockSpec((3, tq, LANES), lambda bi, g, qi: (0, qi, 0)),
        ],
        out_specs=pl.BlockSpec((1, tq, w), lambda bi, g, qi: (bi, qi, g)),
        out_shape=jax.ShapeDtypeStruct((b, s, BRANCH_WIDTH), BF16),
        scratch_shapes=[pltpu.VMEM((s, w), BF16), pltpu.VMEM((s, w), BF16)],
        compiler_params=_cparams(("parallel", "parallel", "arbitrary")),
        name="mla_attn",
    )(proj4, proj4, proj4, proj4, wq, wk, wv, rope_d)


def _mem_attn_kernel(mem_ref, g_ref, w_ref, q_ref, z_ref, o_ref, kv_sc):
    @pl.when(pl.program_id(1) == 0)
    def _():
        m = mem_ref[0]
        ms = jnp.mean(m * m, axis=-1, keepdims=True)
        mn = (m * lax.rsqrt(ms + EPS) * g_ref[...]).astype(BF16)
        kv_sc[...] = jnp.dot(mn, w_ref[...], preferred_element_type=F32).astype(BF16)

    kv_off = MEM_HEADS * MEM_HEAD_DIM

    def scores(h):
        sl = slice(h * MEM_HEAD_DIM, (h + 1) * MEM_HEAD_DIM)
        return lax.dot_general(q_ref[0, 0, :, sl], kv_sc[:, sl], _NT, preferred_element_type=F32)

    def finish(h, s):
        sl = slice(h * MEM_HEAD_DIM, (h + 1) * MEM_HEAD_DIM)
        e, r = _softmax_parts(s)
        vs = slice(kv_off + h * MEM_HEAD_DIM, kv_off + (h + 1) * MEM_HEAD_DIM)
        o = jnp.dot(e.astype(BF16), kv_sc[:, vs], preferred_element_type=F32) * r
        o_ref[0, :, sl] = (o * z_ref[0, 0, :, sl].astype(F32)).astype(BF16)

    s_next = scores(0)
    for h in range(MEM_HEADS):
        s_cur = s_next
        if h + 1 < MEM_HEADS:
            s_next = scores(h + 1)
        finish(h, s_cur)


def _mem_attn(proj4, mem, mem_norm, w_kv, tq):
    _, b, s, _ = proj4.shape
    return pl.pallas_call(
        _mem_attn_kernel,
        grid=(b, s // tq),
        in_specs=[
            pl.BlockSpec((1, N_MEM, D_MODEL), lambda bi, qi: (bi, 0, 0)),
            pl.BlockSpec((1, D_MODEL), lambda bi, qi: (0, 0)),
            pl.BlockSpec((D_MODEL, 2 * MEM_HEADS * MEM_HEAD_DIM), lambda bi, qi: (0, 0)),
            pl.BlockSpec((1, 1, tq, CHUNK), lambda bi, qi: (C_EQ, bi, qi, 0)),
            pl.BlockSpec((1, 1, tq, CHUNK), lambda bi, qi: (C_EZ, bi, qi, 0)),
        ],
        out_specs=pl.BlockSpec((1, tq, CHUNK), lambda bi, qi: (bi, qi, 0)),
        out_shape=jax.ShapeDtypeStruct((b, s, BRANCH_WIDTH), BF16),
        scratch_shapes=[pltpu.VMEM((N_MEM, 2 * MEM_HEADS * MEM_HEAD_DIM), BF16)],
        compiler_params=_cparams(("parallel", "arbitrary")),
        name="mem_attn",
    )(mem, mem_norm, w_kv, proj4, proj4)


def _sconv_kernel(b_ref, c_ref, x_ref, z_ref, w_ref, o_ref, pad_ref):
    s = b_ref.shape[2]
    g = c_ref[0, 0].astype(F32) * x_ref[0, 0].astype(F32)
    zeros = jnp.zeros((SUBLANES, LANES), F32)
    pad_ref[0:SUBLANES] = zeros
    pad_ref[SUBLANES:SUBLANES + s] = g
    pad_ref[SUBLANES + s:2 * SUBLANES + s] = zeros
    w = w_ref[...]
    y = (w[0:1] * pad_ref[SUBLANES - 1:SUBLANES - 1 + s] + w[1:2] * g
         + w[2:3] * pad_ref[SUBLANES + 1:SUBLANES + 1 + s])
    o_ref[0] = (b_ref[0, 0].astype(F32) * y * z_ref[0, 0].astype(F32)).astype(BF16)


def _sconv(proj4, sconv_w):
    _, b, s, _ = proj4.shape
    u = CHUNK // LANES
    return pl.pallas_call(
        _sconv_kernel,
        grid=(b, u),
        in_specs=[
            pl.BlockSpec((1, 1, s, LANES), lambda bi, c: (C_BB, bi, 0, c)),
            pl.BlockSpec((1, 1, s, LANES), lambda bi, c: (C_BC, bi, 0, c)),
            pl.BlockSpec((1, 1, s, LANES), lambda bi, c: (C_BX, bi, 0, c)),
            pl.BlockSpec((1, 1, s, LANES), lambda bi, c: (C_BZ, bi, 0, c)),
            pl.BlockSpec((SC_KERNEL, LANES), lambda bi, c: (0, c)),
        ],
        out_specs=pl.BlockSpec((1, s, LANES), lambda bi, c: (bi, 0, c)),
        out_shape=jax.ShapeDtypeStruct((b, s, BRANCH_WIDTH), BF16),
        scratch_shapes=[pltpu.VMEM((s + 2 * SUBLANES, LANES), F32)],
        compiler_params=_cparams(("parallel", "parallel")),
        name="sconv",
    )(proj4, proj4, proj4, proj4, sconv_w)


CONF_HALO = 16
CONF_ROWS = 256


def _conf_kernel(ga_ref, gb_ref, z_ref, w_ref, b_ref, lg_ref, lb_ref, o_ref, pad_ref):
    s = ga_ref.shape[2]
    zeros = jnp.zeros((CONF_HALO, BRANCH_WIDTH), F32)
    pad_ref[0:CONF_HALO] = zeros
    pad_ref[CONF_HALO:CONF_HALO + s] = ga_ref[0, 0].astype(F32) * gb_ref[0, 0].astype(F32)
    pad_ref[CONF_HALO + s:2 * CONF_HALO + s] = zeros
    first = CONF_HALO - CONF_KERNEL // 2
    rows = CONF_ROWS + 2 * CONF_HALO

    def step(i, carry):
        r0 = pl.multiple_of(i * CONF_ROWS, CONF_ROWS)
        win = pad_ref[pl.ds(r0, rows), :]
        acc = jnp.zeros((CONF_ROWS, BRANCH_WIDTH), F32) + b_ref[...]
        for r in range(SUBLANES):
            shifted = win if r == 0 else pltpu.roll(win, rows - r, 0)
            for k in range(CONF_KERNEL):
                t = first + k
                if t % SUBLANES == r:
                    m = t - r
                    acc = acc + w_ref[k:k + 1, :] * shifted[m:m + CONF_ROWS]
        mu = jnp.mean(acc, axis=-1, keepdims=True)
        d = acc - mu
        var = jnp.mean(d * d, axis=-1, keepdims=True)
        yn = d * lax.rsqrt(var + EPS) * lg_ref[...] + lb_ref[...]
        oc = yn * _sigmoid(yn)
        o_ref[0, pl.ds(r0, CONF_ROWS), :] = (oc * z_ref[0, 0, pl.ds(r0, CONF_ROWS), :].astype(F32)).astype(BF16)
        return carry

    lax.fori_loop(0, s // CONF_ROWS, step, 0)


def _conf(proj4, conf_dw_w, conf_dw_b, conf_ln_g, conf_ln_b):
    _, b, s, _ = proj4.shape
    vec = pl.BlockSpec((1, BRANCH_WIDTH), lambda bi: (0, 0))
    return pl.pallas_call(
        _conf_kernel,
        grid=(b,),
        in_specs=[
            pl.BlockSpec((1, 1, s, CHUNK), lambda bi: (C_GA, bi, 0, 0)),
            pl.BlockSpec((1, 1, s, CHUNK), lambda bi: (C_GB, bi, 0, 0)),
            pl.BlockSpec((1, 1, s, CHUNK), lambda bi: (C_CZ, bi, 0, 0)),
            pl.BlockSpec((CONF_KERNEL, BRANCH_WIDTH), lambda bi: (0, 0)),
            vec, vec, vec,
        ],
        out_specs=pl.BlockSpec((1, s, CHUNK), lambda bi: (bi, 0, 0)),
        out_shape=jax.ShapeDtypeStruct((b, s, BRANCH_WIDTH), BF16),
        scratch_shapes=[pltpu.VMEM((s + 2 * CONF_HALO, BRANCH_WIDTH), F32)],
        compiler_params=_cparams(("parallel",)),
        name="conf",
    )(proj4, proj4, proj4, conf_dw_w, conf_dw_b, conf_ln_g, conf_ln_b)


def _merge_kernel(oa_ref, ob_ref, oc_ref, od_ref, oe_ref, g0, g1, g2, g3, g4, x_ref, wb_ref, wo_ref, n_ref, y_ref):
    outs = (oa_ref, ob_ref, oc_ref, od_ref, oe_ref)
    gates = (g0, g1, g2, g3, g4)
    halves = []
    for c in range(D_MODEL // CHUNK):
        cols = slice(c * CHUNK, (c + 1) * CHUNK)
        y = None
        for i in range(N_BRANCHES):
            t = gates[i][c].astype(F32) * jnp.dot(outs[i][...], wb_ref[i, :, cols], preferred_element_type=F32)
            y = t if y is None else y + t
        halves.append(y.astype(BF16))
    z = jnp.dot(jnp.concatenate(halves, axis=-1), wo_ref[...], preferred_element_type=F32)
    ms = jnp.mean(z * z, axis=-1, keepdims=True)
    y_ref[...] = x_ref[...] + z * lax.rsqrt(ms + EPS) * n_ref[...]


def _merge(outs2d, proj3, x2d, w_branch, w_out, norm_post, tm):
    n = x2d.shape[0]
    o_spec = pl.BlockSpec((tm, BRANCH_WIDTH), lambda i: (i, 0))
    gate_specs = [pl.BlockSpec((D_MODEL // CHUNK, tm, CHUNK), functools.partial(lambda i, c: (c, i, 0), c=c))
                  for c in range(N_BRANCHES)]
    resident = dict(pipeline_mode=pl.Buffered(1))
    return pl.pallas_call(
        _merge_kernel,
        grid=(n // tm,),
        in_specs=[o_spec] * N_BRANCHES + gate_specs + [
            pl.BlockSpec((tm, D_MODEL), lambda i: (i, 0)),
            pl.BlockSpec((N_BRANCHES, BRANCH_WIDTH, D_MODEL), lambda i: (0, 0, 0), **resident),
            pl.BlockSpec((D_MODEL, D_MODEL), lambda i: (0, 0), **resident),
            pl.BlockSpec((1, D_MODEL), lambda i: (0, 0)),
        ],
        out_specs=pl.BlockSpec((tm, D_MODEL), lambda i: (i, 0)),
        out_shape=jax.ShapeDtypeStruct((n, D_MODEL), F32),
        compiler_params=_cparams(("parallel",)),
        name="merge",
    )(*outs2d, *([proj3] * N_BRANCHES), x2d, w_branch, w_out, norm_post)


def _rope_table(seq, rd, theta, period, lane0):
    inv = jnp.float32(theta) ** (-(jnp.arange(0, rd, 2, dtype=F32) / rd))
    ang = jnp.arange(seq, dtype=F32)[:, None] * inv[None, :]
    cos, sin = jnp.cos(ang), jnp.sin(ang)
    half = rd // 2
    pos = (jnp.arange(LANES) % period) - lane0
    idx = jnp.clip(pos % half, 0, half - 1)
    first = (pos >= 0) & (pos < half)
    second = (pos >= half) & (pos < rd)
    cos_l, sin_l = cos[:, idx], sin[:, idx]
    c = jnp.where(first | second, cos_l, 1.0)
    up = jnp.where(first, -sin_l, 0.0)
    dn = jnp.where(second, sin_l, 0.0)
    return jnp.stack([c, up, dn]).astype(F32)


def _prep_layer(w_in, mla_w_uq, mla_w_ukv, mem_w_kv, w_branch, w_out):
    split = [512, 1024, 1536, 2048, 2560, 3072, 3584, 4096, 5120, 5632, 5888, 6016, 6048, 6560, 7072, 7584]
    kr0, kr1, g0 = split[11], split[12], split[15]
    rows = w_in.shape[0]
    w_perm = jnp.concatenate([
        w_in[:, g0:], w_in[:, :kr0],
        jnp.zeros((rows, MLA_ROT_LANE), w_in.dtype), w_in[:, kr0:kr1],
        jnp.zeros((rows, LANES - MLA_ROT_LANE - MLA_ROPE), w_in.dtype),
        w_in[:, kr1:g0]], axis=1).astype(BF16)
    w_perm = w_perm.reshape(rows, N_CHUNKS, CHUNK).transpose(1, 0, 2)
    dq = MLA_NOPE + MLA_ROPE
    wq = mla_w_uq.reshape(MLA_Q_RANK, MLA_HEADS, dq).transpose(1, 0, 2)
    wq = jnp.pad(wq, ((0, 0), (0, 0), (0, LANES - dq))).astype(BF16)
    wkv = mla_w_ukv.reshape(MLA_KV_RANK, MLA_HEADS, MLA_NOPE + MLA_V).transpose(1, 0, 2)
    wk = jnp.pad(wkv[:, :, :MLA_NOPE], ((0, 0), (0, 0), (0, LANES - MLA_NOPE))).astype(BF16)
    wv = wkv[:, :, MLA_NOPE:].astype(BF16)
    return w_perm, wq, wk, wv, mem_w_kv.astype(BF16), w_branch.astype(BF16), w_out.astype(BF16)


def _layer(x, mem, li, prep, rope_a, rope_d, norm_pre, diff_lambda, diff_subln, sconv_w, conf_dw_w, conf_dw_b,
           conf_ln_g, conf_ln_b, mla_q_norm, mla_kv_norm, mem_norm, norm_post):
    w_perm, wq, wk, wv, w_kv, w_branch, w_out = prep
    b, s, d = x.shape
    row = lambda v: v.reshape(1, -1)
    x2d = x.reshape(b * s, d)
    proj3 = _in_proj(x2d, s, row(norm_pre), w_perm, rope_a, rope_d, row(mla_q_norm), row(mla_kv_norm))
    proj4 = proj3.reshape(N_CHUNKS, b, s, CHUNK)
    lam_init = 0.8 - 0.6 * math.exp(-0.3 * li)
    tq = min(ATTN_Q_ROWS, s)
    o_a = _diff_attn(proj4, diff_lambda, row(diff_subln), lam_init, tq, ATTN_HEADS_PER_STEP)
    o_b = _sconv(proj4, sconv_w)
    o_c = _conf(proj4, conf_dw_w, row(conf_dw_b), row(conf_ln_g), row(conf_ln_b))
    o_d = _mla_attn(proj4, wq, wk, wv, rope_d, tq, ATTN_HEADS_PER_STEP)
    o_e = _mem_attn(proj4, mem, row(mem_norm), w_kv, min(MEM_Q_ROWS, s))
    outs2d = [o.reshape(b * s, BRANCH_WIDTH) for o in (o_a, o_b, o_c, o_d, o_e)]
    y2d = _merge(outs2d, proj3, x2d, w_branch, w_out, row(norm_post), min(512, s))
    return y2d.reshape(b, s, d)


def kernel(x_prompt, x_sample, mem_prompt, mem_sample, norm_pre, w_in, diff_lambda, diff_subln, sconv_w, conf_dw_w, conf_dw_b, conf_ln_g, conf_ln_b, mla_q_norm, mla_w_uq, mla_kv_norm, mla_w_ukv, mem_norm, mem_w_kv, w_branch, w_out, norm_post):
    depth = w_in.shape[0]
    y_prompt, y_sample = x_prompt, x_sample
    tables = {}
    for s in {x_prompt.shape[1], x_sample.shape[1]}:
        tables[s] = (_rope_table(s, PARTIAL_ROPE_DIM, ROPE_THETA, DIFF_HEAD_DIM, 0),
                     _rope_table(s, MLA_ROPE, MLA_ROPE_THETA, LANES, MLA_ROT_LANE))
    for li in range(depth):
        prep = _prep_layer(w_in[li], mla_w_uq[li], mla_w_ukv[li], mem_w_kv[li], w_branch[li], w_out[li])
        small = (norm_pre[li], diff_lambda[li], diff_subln[li], sconv_w[li], conf_dw_w[li], conf_dw_b[li],
                 conf_ln_g[li], conf_ln_b[li], mla_q_norm[li], mla_kv_norm[li], mem_norm[li], norm_post[li])
        y_prompt = _layer(y_prompt, mem_prompt, li, prep, *tables[y_prompt.shape[1]], *small)
        y_sample = _layer(y_sample, mem_sample, li, prep, *tables[y_sample.shape[1]], *small)
    return (y_prompt, y_sample)
```
